```python
import math
import jax, jax.numpy as jnp
from jax import lax
import numpy as np

D_MODEL = 1024
BATCH = 8
SEQ = 2048
DEPTH = 1

CHUNK = 64
N_META = 16
META_PAD = CHUNK - N_META

GDN_HEADS = 8
GDN_DK = 128
GDN_DV = 128
GDN_CONV = 4
GDN_QK_W = GDN_HEADS * GDN_DK
GDN_V_W = GDN_HEADS * GDN_DV

SWA_HEADS = 16
SWA_KV_HEADS = 4
SWA_GROUPS = SWA_HEADS // SWA_KV_HEADS
SWA_HD = 64
SWA_Q_W = SWA_HEADS * SWA_HD
SWA_KV_W = SWA_KV_HEADS * SWA_HD
WINDOW = 128
WINDOW_CHUNKS = WINDOW // CHUNK

N_BRANCH = 2
IN_WIDTHS = (GDN_QK_W, GDN_QK_W, GDN_V_W, GDN_V_W, GDN_HEADS, GDN_HEADS,
             SWA_Q_W, SWA_KV_W, SWA_KV_W, SWA_Q_W, D_MODEL, D_MODEL)
IN_VALUE_SEGMENTS = (2, 8)

DEEPNORM_ALPHA = (2.0 * DEPTH) ** 0.25
DEEPNORM_BETA = (8.0 * DEPTH) ** -0.25
LN_EPS = 1e-5
RMS_EPS = 1e-6
L2_EPS = 1e-6

kernel_name = "hybrid_gdn_swa_sink_alibi_metatoken_deepnorm"


def layer_norm(x, w, b):
    xf = x.astype(jnp.float32)
    mu = jnp.mean(xf, axis=-1, keepdims=True)
    var = jnp.mean(jnp.square(xf - mu), axis=-1, keepdims=True)
    y = (xf - mu) * lax.rsqrt(var + LN_EPS) * w.astype(jnp.float32) + b.astype(jnp.float32)
    return y.astype(x.dtype)


def l2_normalize(x):
    return x * lax.rsqrt(jnp.sum(jnp.square(x), axis=-1, keepdims=True) + L2_EPS)


def causal_depthwise_conv(x, w):
    ch = x.shape[-1]
    return lax.conv_general_dilated(
        x, w[:, None, :].astype(x.dtype), window_strides=(1,),
        padding=[(GDN_CONV - 1, 0)], dimension_numbers=('NWC', 'WIO', 'NWC'),
        feature_group_count=ch)


def alibi_slopes(n_heads):
    return jnp.exp2(-8.0 * jnp.arange(1, n_heads + 1, dtype=jnp.float32) / n_heads)


def gated_deltanet(q, k, v, beta_raw, a_raw, conv_w, a_log, dt_bias):
    f32 = jnp.float32
    b, L, _ = q.shape
    qkv = jax.nn.silu(causal_depthwise_conv(jnp.concatenate([q, k, v], axis=-1), conv_w))
    q, k, v = jnp.split(qkv.astype(f32), [GDN_QK_W, 2 * GDN_QK_W], axis=-1)
    q = l2_normalize(q.reshape(b, L, GDN_HEADS, GDN_DK)) * (GDN_DK ** -0.5)
    k = l2_normalize(k.reshape(b, L, GDN_HEADS, GDN_DK))
    v = v.reshape(b, L, GDN_HEADS, GDN_DV)
    beta = jax.nn.sigmoid(beta_raw.astype(f32))
    g = -jnp.exp(a_log.astype(f32)) * jax.nn.softplus(a_raw.astype(f32) + dt_bias.astype(f32))

    pad4 = ((0, 0), (META_PAD, 0), (0, 0), (0, 0))
    pad3 = ((0, 0), (META_PAD, 0), (0, 0))
    q, k, v = jnp.pad(q, pad4), jnp.pad(k, pad4), jnp.pad(v, pad4)
    beta, g = jnp.pad(beta, pad3), jnp.pad(g, pad3)
    n_ch = (L + META_PAD) // CHUNK

    def to_chunks(t):
        return t.reshape(b, n_ch, CHUNK, GDN_HEADS, -1).transpose(0, 3, 1, 2, 4)

    q, k, v = to_chunks(q), to_chunks(k), to_chunks(v)
    beta = beta.reshape(b, n_ch, CHUNK, GDN_HEADS).transpose(0, 3, 1, 2)
    g_cum = jnp.cumsum(g.reshape(b, n_ch, CHUNK, GDN_HEADS).transpose(0, 3, 1, 2), axis=-1)

    tri = jnp.tril(jnp.ones((CHUNK, CHUNK), dtype=bool))
    strict = jnp.tril(jnp.ones((CHUNK, CHUNK), dtype=bool), k=-1)
    decay = jnp.exp(jnp.where(tri, g_cum[..., :, None] - g_cum[..., None, :], -jnp.inf))

    k_beta = k * beta[..., None]
    v_beta = v * beta[..., None]
    lower = jnp.where(strict, jnp.einsum('bhnid,bhnjd->bhnij', k_beta, k) * decay, 0.0)
    rhs = jnp.concatenate([v_beta, k_beta * jnp.exp(g_cum)[..., None]], axis=-1)
    sol = lax.linalg.triangular_solve(lower + jnp.eye(CHUNK, dtype=f32), rhs,
                                      left_side=True, lower=True, unit_diagonal=True)
    u, w = sol[..., :GDN_DV], sol[..., GDN_DV:]
    attn = jnp.einsum('bhnid,bhnjd->bhnij', q, k) * decay
    q_dec = q * jnp.exp(g_cum)[..., None]
    k_dec = k * jnp.exp(g_cum[..., -1:] - g_cum)[..., None]
    g_last = jnp.exp(g_cum[..., -1])

    def step(S, inp):
        u_c, w_c, attn_c, q_c, k_c, gl_c = inp
        v_new = u_c - jnp.einsum('bhik,bhkv->bhiv', w_c, S)
        o_c = jnp.einsum('bhik,bhkv->bhiv', q_c, S) + jnp.einsum('bhij,bhjv->bhiv', attn_c, v_new)
        S = S * gl_c[..., None, None] + jnp.einsum('bhik,bhiv->bhkv', k_c, v_new)
        return S, o_c

    xs = tuple(jnp.moveaxis(t, 2, 0) for t in (u, w, attn, q_dec, k_dec, g_last))
    S0 = jnp.zeros((b, GDN_HEADS, GDN_DK, GDN_DV), f32)
    _, o = lax.scan(step, S0, xs)
    o = o.transpose(1, 0, 3, 2, 4).reshape(b, n_ch * CHUNK, GDN_HEADS, GDN_DV)
    return o[:, META_PAD:]


def sliding_window_sink_attention(q, k, v, sinks):
    f32 = jnp.float32
    b, L, _ = q.shape
    n_ch = (L + META_PAD) // CHUNK
    pad = ((0, 0), (META_PAD, 0), (0, 0))
    q = jnp.pad(q, pad).reshape(b, n_ch, CHUNK, SWA_KV_HEADS, SWA_GROUPS, SWA_HD).transpose(0, 3, 4, 1, 2, 5)
    k = jnp.pad(k, pad).reshape(b, n_ch, CHUNK, SWA_KV_HEADS, SWA_HD).transpose(0, 3, 1, 2, 4)
    v = jnp.pad(v, pad).reshape(b, n_ch, CHUNK, SWA_KV_HEADS, SWA_HD).transpose(0, 3, 1, 2, 4)

    def band(t):
        meta = jnp.broadcast_to(t[:, :, :1, META_PAD:], (b, SWA_KV_HEADS, n_ch, N_META, SWA_HD))
        tp = jnp.pad(t, ((0, 0), (0, 0), (WINDOW_CHUNKS, 0), (0, 0), (0, 0)))
        shifted = [tp[:, :, j:j + n_ch] for j in range(WINDOW_CHUNKS + 1)]
        return jnp.concatenate([meta] + shifted, axis=3)

    k_w, v_w = band(k), band(v)
    scores = jnp.einsum('bkgncd,bknsd->bkgncs', q, k_w).astype(f32) * (SWA_HD ** -0.5)

    c_idx = jnp.arange(n_ch)[:, None]
    t_q = c_idx * CHUNK + jnp.arange(CHUNK)[None, :] - META_PAD
    off = jnp.arange((WINDOW_CHUNKS + 1) * CHUNK)[None, :]
    key_chunk = c_idx - WINDOW_CHUNKS + off // CHUNK
    t_band = key_chunk * CHUNK + off % CHUNK - META_PAD
    t_k = jnp.concatenate([jnp.broadcast_to(jnp.arange(N_META)[None, :], (n_ch, N_META)), t_band], axis=1)
    valid = jnp.concatenate([jnp.ones((n_ch, N_META), dtype=bool), key_chunk >= 1], axis=1)
    dist = jnp.abs(t_q[:, :, None] - t_k[:, None, :]).astype(f32)

    slopes = alibi_slopes(SWA_HEADS).reshape(SWA_KV_HEADS, SWA_GROUPS)
    scores = scores - slopes[:, :, None, None, None] * dist
    scores = jnp.where(valid[:, None, :], scores, -jnp.inf)
    sink = jnp.broadcast_to(sinks.astype(f32).reshape(SWA_KV_HEADS, SWA_GROUPS, 1, 1, 1),
                            scores.shape[:-1] + (1,))
    probs = jax.nn.softmax(jnp.concatenate([scores, sink], axis=-1), axis=-1)[..., :-1]
    o = jnp.einsum('bkgncs,bknsd->bkgncd', probs.astype(v_w.dtype), v_w)
    o = o.transpose(0, 3, 4, 1, 2, 5).reshape(b, n_ch * CHUNK, SWA_Q_W)
    return o[:, META_PAD:]


def hybrid_layer(h, w_in, b_gate, conv_w, a_log, dt_bias, gdn_norm_w, sinks,
                 w_proj_a, w_proj_b, w_out, ln_w, ln_b):
    b, L, _ = h.shape
    offsets, acc = [], 0
    for wd in IN_WIDTHS[:-1]:
        acc += wd
        offsets.append(acc)
    (q_a, k_a, v_a, z_a, beta_a, dec_a, q_b, k_b, v_b, z_b, gate_a, gate_b) = jnp.split(
        h @ w_in, offsets, axis=-1)

    o_a = gated_deltanet(q_a, k_a, v_a, beta_a, dec_a, conv_w, a_log, dt_bias)
    o_a = o_a * lax.rsqrt(jnp.mean(jnp.square(o_a), axis=-1, keepdims=True) + RMS_EPS) * gdn_norm_w.astype(jnp.float32)
    y_a = (o_a.reshape(b, L, GDN_V_W) * jax.nn.silu(z_a.astype(jnp.float32))).astype(h.dtype)

    y_b = sliding_window_sink_attention(q_b, k_b, v_b, sinks) * jax.nn.silu(z_b)

    g_a = jax.nn.sigmoid(gate_a + b_gate[:D_MODEL])
    g_b = jax.nn.sigmoid(gate_b + b_gate[D_MODEL:])
    mixed = g_a * (y_a @ w_proj_a) + g_b * (y_b @ w_proj_b)
    out = mixed @ w_out
    return layer_norm(DEEPNORM_ALPHA * h + out, ln_w, ln_b)


def setup_inputs(seed: int = 0) -> dict:
    key = jax.random.key(seed)
    ks = jax.random.split(key, 16)
    f32 = jnp.float32
    x = jax.random.normal(ks[0], (BATCH, SEQ, D_MODEL), f32)
    meta_tokens = jax.random.normal(ks[1], (N_META, D_MODEL), f32)
    seg_keys = jax.random.split(ks[2], len(IN_WIDTHS))
    segs = []
    for i, (sk, wd) in enumerate(zip(seg_keys, IN_WIDTHS)):
        scale = D_MODEL ** -0.5 * (DEEPNORM_BETA if i in IN_VALUE_SEGMENTS else 1.0)
        segs.append(jax.random.normal(sk, (DEPTH, D_MODEL, wd), f32) * scale)
    w_in = jnp.concatenate(segs, axis=2)
    b_gate = 0.01 * jax.random.normal(ks[3], (DEPTH, N_BRANCH * D_MODEL), f32)
    conv_w = jax.random.normal(ks[4], (DEPTH, GDN_CONV, 2 * GDN_QK_W + GDN_V_W), f32) * (GDN_CONV ** -0.5)
    a_log = jnp.log(jax.random.uniform(ks[5], (DEPTH, GDN_HEADS), f32, minval=1.0, maxval=16.0))
    dt = jnp.exp(jax.random.uniform(ks[6], (DEPTH, GDN_HEADS), f32,
                                    minval=math.log(1e-3), maxval=math.log(1e-1)))
    dt_bias = dt + jnp.log(-jnp.expm1(-dt))
    gdn_norm_w = 1.0 + 0.02 * jax.random.normal(ks[7], (DEPTH, GDN_DV), f32)
    sinks = jax.random.normal(ks[8], (DEPTH, SWA_HEADS), f32)
    w_proj_a = jax.random.normal(ks[9], (DEPTH, GDN_V_W, D_MODEL), f32) * (GDN_V_W ** -0.5 * DEEPNORM_BETA)
    w_proj_b = jax.random.normal(ks[10], (DEPTH, SWA_Q_W, D_MODEL), f32) * (SWA_Q_W ** -0.5 * DEEPNORM_BETA)
    w_out = jax.random.normal(ks[11], (DEPTH, D_MODEL, D_MODEL), f32) * (D_MODEL ** -0.5 * DEEPNORM_BETA)
    ln_w = 1.0 + 0.02 * jax.random.normal(ks[12], (DEPTH, D_MODEL), f32)
    ln_b = 0.02 * jax.random.normal(ks[13], (DEPTH, D_MODEL), f32)
    return {"x": x, "meta_tokens": meta_tokens, "w_in": w_in, "b_gate": b_gate,
            "conv_w": conv_w, "a_log": a_log, "dt_bias": dt_bias, "gdn_norm_w": gdn_norm_w,
            "sinks": sinks, "w_proj_a": w_proj_a, "w_proj_b": w_proj_b, "w_out": w_out,
            "ln_w": ln_w, "ln_b": ln_b}


def reference(x, meta_tokens, w_in, b_gate, conv_w, a_log, dt_bias, gdn_norm_w, sinks,
              w_proj_a, w_proj_b, w_out, ln_w, ln_b):
    b = x.shape[0]
    meta = jnp.broadcast_to(meta_tokens.astype(x.dtype)[None], (b, N_META, D_MODEL))
    h = jnp.concatenate([meta, x], axis=1)
    for l in range(DEPTH):
        h = hybrid_layer(h, w_in[l], b_gate[l], conv_w[l], a_log[l], dt_bias[l], gdn_norm_w[l],
                         sinks[l], w_proj_a[l], w_proj_b[l], w_out[l], ln_w[l], ln_b[l])
    return h[:, N_META:]
```

```python
import functools
import math

import jax
import jax.numpy as jnp
from jax import lax
from jax.experimental import pallas as pl
from jax.experimental.pallas import tpu as pltpu

F32 = jnp.float32
BF16 = jnp.bfloat16

D_MODEL = 1024
CHUNK = 64
N_META = 16
META_PAD = CHUNK - N_META
GDN_HEADS = 8
GDN_D = 128
GDN_CONV = 4
GDN_W = GDN_HEADS * GDN_D
SWA_HEADS = 16
SWA_KV_HEADS = 4
SWA_GROUPS = SWA_HEADS // SWA_KV_HEADS
SWA_HD = 64
SWA_KV_W = SWA_KV_HEADS * SWA_HD
WINDOW_CHUNKS = 2
DEEPNORM_ALPHA = 2.0 ** 0.25
LN_EPS = 1e-5
RMS_EPS = 1e-6
L2_EPS = 1e-6

COL_QKV_A = 0
COL_Z_A = 3 * GDN_W
COL_Q_B = 4 * GDN_W
COL_Z_B = COL_Q_B + 1024
COL_GATE_A = COL_Z_B + 1024
COL_GATE_B = COL_GATE_A + 1024
COL_KV_B = COL_GATE_B + 1024
PROJ_W = COL_KV_B + 2 * SWA_KV_W
BD_W = 128

PROJ_TN = 512
HALO = 16
VMEM_LIMIT = 48 * 1024 * 1024


def _mm(a, b):
    return jnp.dot(a.astype(BF16), b.astype(BF16), preferred_element_type=F32)


def _mm_nt(a, b):
    return lax.dot_general(a.astype(BF16), b.astype(BF16), (((1,), (1,)), ((), ())),
                           preferred_element_type=F32)


def _mm_tn(a, b):
    return lax.dot_general(a.astype(BF16), b.astype(BF16), (((0,), (0,)), ((), ())),
                           preferred_element_type=F32)


def _mm_f32(a, b):
    return jnp.dot(a, b, preferred_element_type=F32, precision=lax.Precision.HIGHEST)


def _mm_tn_f32(a, b):
    return lax.dot_general(a, b, (((0,), (0,)), ((), ())), preferred_element_type=F32,
                           precision=lax.Precision.HIGHEST)


def _silu(x):
    return x * (1.0 / (1.0 + jnp.exp(-x)))


def _sigmoid(x):
    return 1.0 / (1.0 + jnp.exp(-x))


def _softplus(x):
    return jnp.maximum(x, 0.0) + jnp.log(1.0 + jnp.exp(-jnp.abs(x)))


def _proj_kernel(x_ref, w_ref, wbd_ref, p_ref, bd_ref, xb_ref):
    @pl.when(pl.program_id(1) == 0)
    def _():
        xb = x_ref[...].astype(BF16)
        xb_ref[...] = xb
        bd_ref[...] = jnp.dot(xb, wbd_ref[...], preferred_element_type=F32)

    p_ref[...] = jnp.dot(xb_ref[...], w_ref[...], preferred_element_type=F32).astype(BF16)


def _project(x2d, w_main, w_bd, tm):
    m = x2d.shape[0]
    return pl.pallas_call(
        _proj_kernel,
        grid=(m // tm, PROJ_W // PROJ_TN),
        in_specs=[
            pl.BlockSpec((tm, D_MODEL), lambda i, j: (i, 0)),
            pl.BlockSpec((D_MODEL, PROJ_TN), lambda i, j: (0, j)),
            pl.BlockSpec((D_MODEL, BD_W), lambda i, j: (0, 0)),
        ],
        out_specs=[
            pl.BlockSpec((tm, PROJ_TN), lambda i, j: (i, j)),
            pl.BlockSpec((tm, BD_W), lambda i, j: (i, 0)),
        ],
        out_shape=[
            jax.ShapeDtypeStruct((m, PROJ_W), BF16),
            jax.ShapeDtypeStruct((m, BD_W), F32),
        ],
        scratch_shapes=[pltpu.VMEM((tm, D_MODEL), BF16)],
        compiler_params=pltpu.CompilerParams(
            dimension_semantics=("arbitrary", "arbitrary"), vmem_limit_bytes=VMEM_LIMIT),
        name="proj",
    )(x2d, w_main, w_bd)


def _tri_inverse(a):
    ri = lax.broadcasted_iota(jnp.int32, (CHUNK, CHUNK), 0)
    ci = lax.broadcasted_iota(jnp.int32, (CHUNK, CHUNK), 1)
    eye = (ri == ci).astype(F32)
    t = eye - jnp.where((ri // 2) == (ci // 2), a, 0.0)
    s = 2
    while s < CHUNK:
        off = ((ri // (2 * s)) == (ci // (2 * s))) & ((ri // s) != (ci // s))
        a_off = jnp.where(off, a, 0.0)
        t = t - _mm_f32(_mm_f32(t, a_off), t)
        s *= 2
    return t


def _gdn_chunk(prev_qkv, cur_qkv, bd, row_valid, convw_ref, alog_ref, dtb_ref, s_ref):
    ri = lax.broadcasted_iota(jnp.int32, (CHUNK, CHUNK), 0)
    ci = lax.broadcasted_iota(jnp.int32, (CHUNK, CHUNK), 1)
    tri = ri >= ci
    strict = ri > ci
    lower_ones = tri.astype(F32)
    upper_ones = (ri <= ci).astype(F32)

    beta = _sigmoid(bd[:, 0:GDN_HEADS])
    g = -jnp.exp(alog_ref[...]) * _softplus(bd[:, GDN_HEADS:2 * GDN_HEADS] + dtb_ref[...])
    g = g * row_valid
    g_cum = _mm_f32(lower_ones, g)
    g_cum_t = _mm_tn_f32(g, upper_ones)

    outs = []
    for h in range(GDN_HEADS):
        def conv_part(part):
            xx = jnp.concatenate([prev_qkv(h, part), cur_qkv(h, part)], axis=0)
            col = part * GDN_W + h * GDN_D
            acc = None
            for j in range(GDN_CONV):
                lo = HALO - (GDN_CONV - 1) + j
                term = xx[lo:lo + CHUNK, :] * convw_ref[j:j + 1, col:col + GDN_D]
                acc = term if acc is None else acc + term
            return _silu(acc)

        q = conv_part(0)
        k = conv_part(1)
        v = conv_part(2)
        q = q * (lax.rsqrt(jnp.sum(q * q, axis=-1, keepdims=True) + L2_EPS) * (GDN_D ** -0.5))
        k = k * lax.rsqrt(jnp.sum(k * k, axis=-1, keepdims=True) + L2_EPS)

        b_col = beta[:, h:h + 1]
        gc_col = g_cum[:, h:h + 1]
        gc_row = g_cum_t[h:h + 1, :]
        gc_last = g_cum_t[h:h + 1, CHUNK - 1:CHUNK]
        decay = jnp.where(tri, jnp.exp(gc_col - gc_row), 0.0)

        k_beta = k * b_col
        v_beta = v * b_col
        e_col = jnp.exp(gc_col)
        a = jnp.where(strict, _mm_nt(k_beta, k) * decay, 0.0)
        t_inv = _tri_inverse(a)
        uw = _mm(t_inv, jnp.concatenate([v_beta, k_beta * e_col], axis=1))
        u, w = uw[:, :GDN_D], uw[:, GDN_D:]
        attn = jnp.where(tri, _mm_nt(q, k) * decay, 0.0)
        q_dec = q * e_col
        k_dec = k * jnp.exp(gc_last - gc_col)

        s = s_ref[h]
        v_new = u - _mm(w, s)
        o = _mm(q_dec, s) + _mm(attn, v_new)
        s_ref[h] = s * jnp.exp(gc_last) + _mm_tn(k_dec, v_new)
        outs.append(o)
    return outs


def _gdn_kernel(pa_ref, halo_ref, pm_ref, bdx_ref, bdm_ref, convw_ref, alog_ref, dtb_ref,
                nw_ref, y_ref, s_ref, *, chunks_per_tile):
    t = pl.program_id(1)

    @pl.when(t == 0)
    def _():
        s_ref[...] = jnp.zeros_like(s_ref)
        zeros = jnp.zeros((HALO, GDN_D), F32)

        def cur(h, part):
            col = part * GDN_W + h * GDN_D
            return pm_ref[:, col:col + GDN_D].astype(F32)

        rows = lax.broadcasted_iota(jnp.int32, (CHUNK, 1), 0)
        _gdn_chunk(lambda h, part: zeros, cur, bdm_ref[...], (rows >= META_PAD).astype(F32),
                   convw_ref, alog_ref, dtb_ref, s_ref)

    ones = jnp.ones((CHUNK, 1), F32)
    for c in range(chunks_per_tile):
        r0 = c * CHUNK

        def cur(h, part, r0=r0):
            col = part * GDN_W + h * GDN_D
            return pa_ref[0, r0:r0 + CHUNK, col:col + GDN_D].astype(F32)

        if c == 0:
            def prev(h, part):
                col = part * GDN_W + h * GDN_D
                from_meta = pm_ref[CHUNK - HALO:CHUNK, col:col + GDN_D].astype(F32)
                from_x = halo_ref[0, :, col:col + GDN_D].astype(F32)
                return jnp.where(t == 0, from_meta, from_x)
        else:
            def prev(h, part, r0=r0):
                col = part * GDN_W + h * GDN_D
                return pa_ref[0, r0 - HALO:r0, col:col + GDN_D].astype(F32)

        outs = _gdn_chunk(prev, cur, bdx_ref[0, r0:r0 + CHUNK, :], ones,
                          convw_ref, alog_ref, dtb_ref, s_ref)
        for h in range(GDN_HEADS):
            o = outs[h]
            o = o * lax.rsqrt(jnp.mean(o * o, axis=-1, keepdims=True) + RMS_EPS) * nw_ref[...]
            z = pa_ref[0, r0:r0 + CHUNK, COL_Z_A + h * GDN_D:COL_Z_A + (h + 1) * GDN_D].astype(F32)
            y_ref[0, r0:r0 + CHUNK, h * GDN_D:(h + 1) * GDN_D] = (o * _silu(z)).astype(BF16)


def _gdn(p_x, p_m, bd_x, bd_m, conv_w, a_log, dt_bias, norm_w, chunks_per_tile=4):
    b, seq, _ = p_x.shape
    tile = chunks_per_tile * CHUNK
    halo_blocks_per_tile = tile // HALO
    return pl.pallas_call(
        functools.partial(_gdn_kernel, chunks_per_tile=chunks_per_tile),
        grid=(b, seq // tile),
        in_specs=[
            pl.BlockSpec((1, tile, 4 * GDN_W), lambda i, t: (i, t, 0)),
            pl.BlockSpec((1, HALO, 3 * GDN_W),
                         lambda i, t: (i, jnp.maximum(t * halo_blocks_per_tile - 1, 0), 0)),
            pl.BlockSpec((CHUNK, 4 * GDN_W), lambda i, t: (0, 0)),
            pl.BlockSpec((1, tile, BD_W), lambda i, t: (i, t, 0)),
            pl.BlockSpec((CHUNK, BD_W), lambda i, t: (0, 0)),
            pl.BlockSpec((GDN_CONV, 3 * GDN_W), lambda i, t: (0, 0)),
            pl.BlockSpec((1, GDN_HEADS), lambda i, t: (0, 0)),
            pl.BlockSpec((1, GDN_HEADS), lambda i, t: (0, 0)),
            pl.BlockSpec((1, GDN_D), lambda i, t: (0, 0)),
        ],
        out_specs=pl.BlockSpec((1, tile, GDN_W), lambda i, t: (i, t, 0)),
        out_shape=jax.ShapeDtypeStruct((b, seq, GDN_W), BF16),
        scratch_shapes=[pltpu.VMEM((GDN_HEADS, GDN_D, GDN_D), F32)],
        compiler_params=pltpu.CompilerParams(
            dimension_semantics=("arbitrary", "arbitrary"), vmem_limit_bytes=VMEM_LIMIT),
        name="gdn",
    )(p_x, p_x, p_m, bd_x, bd_m, conv_w, a_log, dt_bias, norm_w)


def _swa_kernel(q_ref, z_ref, kv_ref, kvm_ref, sinks_ref, y_ref, *, chunks_per_tile):
    t = pl.program_id(1)
    band = (WINDOW_CHUNKS + 1) * CHUNK
    n_keys = N_META + band
    rows = SWA_GROUPS * CHUNK

    qi = lax.broadcasted_iota(jnp.int32, (rows, n_keys), 0)
    kj = lax.broadcasted_iota(jnp.int32, (rows, n_keys), 1)
    q_in_chunk = qi % CHUNK
    group = lax.broadcasted_iota(jnp.int32, (rows, 1), 0) // CHUNK
    is_meta = kj < N_META
    band_chunk = (kj - N_META) // CHUNK
    rel_meta = q_in_chunk + N_META - kj
    rel_band = q_in_chunk - (kj - N_META)

    for c in range(chunks_per_tile):
        r0 = c * CHUNK
        m = t * chunks_per_tile + c
        ws = jnp.maximum(m - WINDOW_CHUNKS, 0)
        k0 = pl.multiple_of(ws * CHUNK, CHUNK)
        dist = jnp.abs(jnp.where(is_meta, rel_meta + m * CHUNK, rel_band + (m - ws) * CHUNK)).astype(F32)
        valid = is_meta | (band_chunk + ws <= m)

        kv_win = kv_ref[0, pl.ds(k0, band), :]
        kv_meta = kvm_ref[META_PAD:CHUNK, :]
        for kvh in range(SWA_KV_HEADS):
            kcol = kvh * SWA_HD
            vcol = SWA_KV_W + kvh * SWA_HD
            k_win = jnp.concatenate([kv_meta[:, kcol:kcol + SWA_HD], kv_win[:, kcol:kcol + SWA_HD]], axis=0)
            v_win = jnp.concatenate([kv_meta[:, vcol:vcol + SWA_HD], kv_win[:, vcol:vcol + SWA_HD]], axis=0)
            q_stack = jnp.concatenate(
                [q_ref[0, r0:r0 + CHUNK, (kvh * SWA_GROUPS + g) * SWA_HD:(kvh * SWA_GROUPS + g + 1) * SWA_HD]
                 for g in range(SWA_GROUPS)], axis=0)
            scores = _mm_nt(q_stack, k_win) * (SWA_HD ** -0.5)
            slope = jnp.zeros((rows, 1), F32)
            sink = jnp.zeros((rows, 1), F32)
            for g in range(SWA_GROUPS):
                head = kvh * SWA_GROUPS + g
                slope = jnp.where(group == g, 2.0 ** (-8.0 * (head + 1) / SWA_HEADS), slope)
                sink = jnp.where(group == g, sinks_ref[0, head], sink)
            scores = jnp.where(valid, scores - slope * dist, -jnp.inf)
            mx = jnp.maximum(jnp.max(scores, axis=-1, keepdims=True), sink)
            e = jnp.exp(scores - mx)
            denom = jnp.sum(e, axis=-1, keepdims=True) + jnp.exp(sink - mx)
            probs = e * (1.0 / denom)
            o = _mm(probs, v_win)
            o = jnp.concatenate([o[g * CHUNK:(g + 1) * CHUNK, :] for g in range(SWA_GROUPS)], axis=1)
            zc = kvh * SWA_GROUPS * SWA_HD
            z = z_ref[0, r0:r0 + CHUNK, zc:zc + SWA_GROUPS * SWA_HD].astype(F32)
            y_ref[0, r0:r0 + CHUNK, zc:zc + SWA_GROUPS * SWA_HD] = (o * _silu(z)).astype(BF16)


def _swa(p_x, p_m, sinks, chunks_per_tile=4):
    b, seq, _ = p_x.shape
    tile = chunks_per_tile * CHUNK
    width = SWA_HEADS * SWA_HD
    return pl.pallas_call(
        functools.partial(_swa_kernel, chunks_per_tile=chunks_per_tile),
        grid=(b, seq // tile),
        in_specs=[
            pl.BlockSpec((1, tile, width), lambda i, t: (i, t, COL_Q_B // width)),
            pl.BlockSpec((1, tile, width), lambda i, t: (i, t, COL_Z_B // width)),
            pl.BlockSpec((1, seq, 2 * SWA_KV_W), lambda i, t: (i, 0, COL_KV_B // (2 * SWA_KV_W))),
            pl.BlockSpec((CHUNK, 2 * SWA_KV_W), lambda i, t: (0, COL_KV_B // (2 * SWA_KV_W))),
            pl.BlockSpec(memory_space=pltpu.SMEM),
        ],
        out_specs=pl.BlockSpec((1, tile, width), lambda i, t: (i, t, 0)),
        out_shape=jax.ShapeDtypeStruct((b, seq, width), BF16),
        compiler_params=pltpu.CompilerParams(
            dimension_semantics=("arbitrary", "arbitrary"), vmem_limit_bytes=VMEM_LIMIT),
        name="swa",
    )(p_x, p_x, p_x, p_m, sinks)


def _out_kernel(x_ref, ya_ref, yb_ref, ga_ref, gb_ref, bg_ref, wa_ref, wb_ref, wo_ref,
                lnw_ref, lnb_ref, o_ref):
    g_a = _sigmoid(ga_ref[...].astype(F32) + bg_ref[:, 0:D_MODEL])
    g_b = _sigmoid(gb_ref[...].astype(F32) + bg_ref[:, D_MODEL:2 * D_MODEL])
    mixed = g_a * _mm(ya_ref[...], wa_ref[...]) + g_b * _mm(yb_ref[...], wb_ref[...])
    r = DEEPNORM_ALPHA * x_ref[...] + _mm(mixed, wo_ref[...])
    mu = jnp.mean(r, axis=-1, keepdims=True)
    d = r - mu
    var = jnp.mean(d * d, axis=-1, keepdims=True)
    o_ref[...] = d * lax.rsqrt(var + LN_EPS) * lnw_ref[...] + lnb_ref[...]


def _output(x2d, y_a, y_b, p_x2d, b_gate, w_a, w_b, w_o, ln_w, ln_b, tm=512):
    m = x2d.shape[0]
    row = lambda i: (i, 0)
    const = lambda i: (0, 0)
    return pl.pallas_call(
        _out_kernel,
        grid=(m // tm,),
        in_specs=[
            pl.BlockSpec((tm, D_MODEL), row),
            pl.BlockSpec((tm, D_MODEL), row),
            pl.BlockSpec((tm, D_MODEL), row),
            pl.BlockSpec((tm, D_MODEL), lambda i: (i, COL_GATE_A // D_MODEL)),
            pl.BlockSpec((tm, D_MODEL), lambda i: (i, COL_GATE_B // D_MODEL)),
            pl.BlockSpec((1, 2 * D_MODEL), const),
            pl.BlockSpec((D_MODEL, D_MODEL), const),
            pl.BlockSpec((D_MODEL, D_MODEL), const),
            pl.BlockSpec((D_MODEL, D_MODEL), const),
            pl.BlockSpec((1, D_MODEL), const),
            pl.BlockSpec((1, D_MODEL), const),
        ],
        out_specs=pl.BlockSpec((tm, D_MODEL), row),
        out_shape=jax.ShapeDtypeStruct((m, D_MODEL), F32),
        compiler_params=pltpu.CompilerParams(
            dimension_semantics=("arbitrary",), vmem_limit_bytes=VMEM_LIMIT),
        name="merge_out",
    )(x2d, y_a, y_b, p_x2d, p_x2d, b_gate, w_a, w_b, w_o, ln_w, ln_b)


def _one_layer(x, meta_tokens, w_in, b_gate, conv_w, a_log, dt_bias, gdn_norm_w, sinks,
               w_proj_a, w_proj_b, w_out, ln_w, ln_b):
    b, seq, _ = x.shape
    o_bd = 4 * GDN_W
    o_qb = o_bd + 2 * GDN_HEADS
    o_kb = o_qb + 1024
    o_zb = o_kb + 2 * SWA_KV_W
    w_main = jnp.concatenate(
        [w_in[:, :o_bd], w_in[:, o_qb:o_kb], w_in[:, o_zb:], w_in[:, o_kb:o_zb]], axis=1).astype(BF16)
    w_bd = jnp.pad(w_in[:, o_bd:o_qb], ((0, 0), (0, BD_W - 2 * GDN_HEADS))).astype(BF16)

    x2d = x.reshape(b * seq, D_MODEL)
    meta_chunk = jnp.concatenate([jnp.zeros((META_PAD, D_MODEL), x.dtype), meta_tokens.astype(x.dtype)], axis=0)

    p_x2d, bd_x2d = _project(x2d, w_main, w_bd, tm=1024)
    p_m, bd_m = _project(meta_chunk, w_main, w_bd, tm=CHUNK)
    p_x = p_x2d.reshape(b, seq, PROJ_W)
    bd_x = bd_x2d.reshape(b, seq, BD_W)

    y_a = _gdn(p_x, p_m, bd_x, bd_m, conv_w, a_log.reshape(1, GDN_HEADS), dt_bias.reshape(1, GDN_HEADS),
               gdn_norm_w.reshape(1, GDN_D))
    y_b = _swa(p_x, p_m, sinks.reshape(1, SWA_HEADS))

    out = _output(x2d, y_a.reshape(b * seq, GDN_W), y_b.reshape(b * seq, SWA_HEADS * SWA_HD), p_x2d,
                  b_gate.reshape(1, 2 * D_MODEL), w_proj_a.astype(BF16), w_proj_b.astype(BF16),
                  w_out.astype(BF16), ln_w.reshape(1, D_MODEL), ln_b.reshape(1, D_MODEL))
    return out.reshape(b, seq, D_MODEL)


def kernel(x, meta_tokens, w_in, b_gate, conv_w, a_log, dt_bias, gdn_norm_w, sinks,
           w_proj_a, w_proj_b, w_out, ln_w, ln_b):
    depth = w_in.shape[0]
    assert depth == 1, "meta tokens are projected once; deeper stacks need per-layer meta rows"
    return _one_layer(x, meta_tokens, w_in[0], b_gate[0], conv_w[0], a_log[0], dt_bias[0],
                      gdn_norm_w[0], sinks[0], w_proj_a[0], w_proj_b[0], w_out[0], ln_w[0], ln_b[0])
```

```python
import functools

import jax
import jax.numpy as jnp
from jax import lax
from jax.experimental import pallas as pl
from jax.experimental.pallas import tpu as pltpu

F32 = jnp.float32
BF16 = jnp.bfloat16

D_MODEL = 1024
CHUNK = 64
N_META = 16
META_PAD = CHUNK - N_META
GDN_HEADS = 8
GDN_D = 128
GDN_CONV = 4
GDN_W = GDN_HEADS * GDN_D
SWA_HEADS = 16
SWA_KV_HEADS = 4
SWA_GROUPS = SWA_HEADS // SWA_KV_HEADS
SWA_HD = 64
SWA_W = SWA_HEADS * SWA_HD
SWA_KV_W = SWA_KV_HEADS * SWA_HD
WINDOW_CHUNKS = 2
DEEPNORM_ALPHA = 2.0 ** 0.25
LN_EPS = 1e-5
RMS_EPS = 1e-6
L2_EPS = 1e-6

COL_Q_A = 0
COL_K_A = GDN_W
COL_V_A = 2 * GDN_W
COL_Z_A = 3 * GDN_W
COL_Q_B = 4 * GDN_W
COL_Z_B = COL_Q_B + SWA_W
COL_GATE_A = COL_Z_B + SWA_W
COL_GATE_B = COL_GATE_A + D_MODEL
COL_KV_B = COL_GATE_B + D_MODEL
PROJ_W = COL_KV_B + 2 * SWA_KV_W
CONV_W = 3 * GDN_W
BD_W = 128

PROJ_TN = 512
PROJ_TM = 512
CONV_HALO = 8
VMEM_LIMIT = 48 * 1024 * 1024
PROJ_VMEM_LIMIT = 56 * 1024 * 1024


def _mm(a, b):
    return jnp.dot(a.astype(BF16), b.astype(BF16), preferred_element_type=F32)


def _mm_nt(a, b):
    return lax.dot_general(a.astype(BF16), b.astype(BF16), (((1,), (1,)), ((), ())),
                           preferred_element_type=F32)


def _mm_tn(a, b):
    return lax.dot_general(a.astype(BF16), b.astype(BF16), (((0,), (0,)), ((), ())),
                           preferred_element_type=F32)


def _mm_f32(a, b):
    return jnp.dot(a, b, preferred_element_type=F32, precision=lax.Precision.HIGHEST)


def _mm_tn_f32(a, b):
    return lax.dot_general(a, b, (((0,), (0,)), ((), ())), preferred_element_type=F32,
                           precision=lax.Precision.HIGHEST)


def _sigmoid(x):
    return 1.0 / (1.0 + jnp.exp(-x))


def _silu(x):
    return x * _sigmoid(x)


def _softplus(x):
    return jnp.maximum(x, 0.0) + jnp.log(1.0 + jnp.exp(-jnp.abs(x)))


def _proj_kernel(x_ref, w_ref, wbd_ref, convw_ref, bg_ref, halo0_ref, p_ref, bd_ref, tail_ref,
                 carry_ref, xb_ref, *, tm, tiles_per_seq):
    first_of_seq = (pl.program_id(0) % tiles_per_seq) == 0
    xb_ref[...] = x_ref[...].astype(BF16)
    bd_ref[...] = jnp.dot(xb_ref[...], wbd_ref[...], preferred_element_type=F32)

    for j in range(PROJ_W // PROJ_TN):
        c0 = j * PROJ_TN
        cols = slice(c0, c0 + PROJ_TN)
        acc = jnp.dot(xb_ref[...], w_ref[:, cols], preferred_element_type=F32)
        if c0 < CONV_W:
            halo = jnp.where(first_of_seq, halo0_ref[:, cols], carry_ref[:, cols])
            tail = acc[tm - CONV_HALO:tm, :]
            carry_ref[:, cols] = tail
            tail_ref[:, cols] = tail
            raw = jnp.concatenate([halo, acc], axis=0)
            y = acc * convw_ref[GDN_CONV - 1:GDN_CONV, cols]
            for back in range(1, GDN_CONV):
                shifted = pltpu.roll(raw, back, axis=0)[CONV_HALO:CONV_HALO + tm, :]
                y = y + shifted * convw_ref[GDN_CONV - 1 - back:GDN_CONV - back, cols]
            y = _silu(y)
            if c0 < COL_V_A:
                post = GDN_D ** -0.5 if c0 < COL_K_A else 1.0
                groups = []
                for g in range(PROJ_TN // GDN_D):
                    yg = y[:, g * GDN_D:(g + 1) * GDN_D]
                    ss = jnp.sum(yg * yg, axis=-1, keepdims=True)
                    groups.append(yg * (lax.rsqrt(ss + L2_EPS) * post))
                y = jnp.concatenate(groups, axis=1)
        elif c0 < COL_Q_B or COL_Z_B <= c0 < COL_GATE_A:
            y = _silu(acc)
        elif c0 < COL_Z_B:
            y = acc * (SWA_HD ** -0.5)
        elif c0 < COL_KV_B:
            y = _sigmoid(acc + bg_ref[:, c0 - COL_GATE_A:c0 - COL_GATE_A + PROJ_TN])
        else:
            y = acc
        p_ref[:, cols] = y.astype(BF16)


def _project(x2d, w_main, w_bd, conv_w, b_gate, halo0, tm, tiles_per_seq):
    m = x2d.shape[0]
    n_tiles = m // tm
    const = lambda i: (0, 0)
    resident = dict(pipeline_mode=pl.Buffered(1))
    return pl.pallas_call(
        functools.partial(_proj_kernel, tm=tm, tiles_per_seq=tiles_per_seq),
        grid=(n_tiles,),
        in_specs=[
            pl.BlockSpec((tm, D_MODEL), lambda i: (i, 0)),
            pl.BlockSpec((D_MODEL, PROJ_W), const, **resident),
            pl.BlockSpec((D_MODEL, BD_W), const, **resident),
            pl.BlockSpec((GDN_CONV, CONV_W), const, **resident),
            pl.BlockSpec((1, 2 * D_MODEL), const, **resident),
            pl.BlockSpec((CONV_HALO, CONV_W), const, **resident),
        ],
        out_specs=[
            pl.BlockSpec((tm, PROJ_W), lambda i: (i, 0)),
            pl.BlockSpec((tm, BD_W), lambda i: (i, 0)),
            pl.BlockSpec((CONV_HALO, CONV_W), lambda i: (i, 0)),
        ],
        out_shape=[
            jax.ShapeDtypeStruct((m, PROJ_W), BF16),
            jax.ShapeDtypeStruct((m, BD_W), F32),
            jax.ShapeDtypeStruct((n_tiles * CONV_HALO, CONV_W), F32),
        ],
        scratch_shapes=[pltpu.VMEM((CONV_HALO, CONV_W), F32), pltpu.VMEM((tm, D_MODEL), BF16)],
        compiler_params=pltpu.CompilerParams(
            dimension_semantics=("arbitrary",), vmem_limit_bytes=PROJ_VMEM_LIMIT),
        name="proj",
    )(x2d, w_main, w_bd, conv_w, b_gate, halo0)


def _gdn_prepare(chunks, alog_ref, dtb_ref):
    ri = lax.broadcasted_iota(jnp.int32, (CHUNK, CHUNK), 0)
    ci = lax.broadcasted_iota(jnp.int32, (CHUNK, CHUNK), 1)
    tri = ri >= ci
    strict = ri > ci
    lower_ones = tri.astype(F32)
    upper_ones = (ri <= ci).astype(F32)
    eye = (ri == ci).astype(F32)
    neg_alpha = -jnp.exp(alog_ref[...])

    gates = []
    for qkv, bd, row_valid in chunks:
        beta = _sigmoid(bd[:, 0:GDN_HEADS])
        g = neg_alpha * _softplus(bd[:, GDN_HEADS:2 * GDN_HEADS] + dtb_ref[...])
        if row_valid is not None:
            g = g * row_valid
        gates.append((beta, _mm_f32(lower_ones, g), _mm_tn_f32(g, upper_ones)))

    inst = []
    for (qkv, bd, row_valid), (beta, g_cum, g_cum_t) in zip(chunks, gates):
        for h in range(GDN_HEADS):
            q = qkv(h, 0).astype(F32)
            k_bf = qkv(h, 1)
            k = k_bf.astype(F32)
            v = qkv(h, 2).astype(F32)
            b_col = beta[:, h:h + 1]
            gc_col = g_cum[:, h:h + 1]
            gc_row = g_cum_t[h:h + 1, :]
            gc_last = g_cum_t[h:h + 1, CHUNK - 1:CHUNK]
            e_col = jnp.exp(gc_col)
            k_beta = k * b_col
            inst.append(dict(
                k=k_bf,
                kb_q=jnp.concatenate([k_beta, q], axis=0).astype(BF16),
                rhs=jnp.concatenate([v * b_col, k_beta * e_col], axis=1).astype(BF16),
                q_dec=(q * e_col).astype(BF16),
                k_dec=(k * jnp.exp(gc_last - gc_col)).astype(BF16),
                decay=jnp.where(tri, jnp.exp(gc_col - gc_row), 0.0),
                g_last=jnp.exp(gc_last)))

    for it in inst:
        it["kq"] = _mm_nt(it["kb_q"], it["k"])
    for it in inst:
        a = jnp.where(strict, it["kq"][:CHUNK] * it["decay"], 0.0)
        it["a"] = a
        it["attn"] = (it["kq"][CHUNK:] * it["decay"]).astype(BF16)
        it["t"] = eye - jnp.where((ri // 2) == (ci // 2), a, 0.0)

    s = 2
    while s < CHUNK:
        off = ((ri // (2 * s)) == (ci // (2 * s))) & ((ri // s) != (ci // s))
        for it in inst:
            it["x"] = _mm(it["t"], jnp.where(off, it["a"], 0.0))
        for it in inst:
            it["x"] = _mm(it["x"], it["t"])
        for it in inst:
            it["t"] = it["t"] - it["x"]
        s *= 2

    for it in inst:
        it["uw"] = _mm(it["t"], it["rhs"])

    out = []
    for c in range(len(chunks)):
        heads = []
        for h in range(GDN_HEADS):
            it = inst[c * GDN_HEADS + h]
            w_q = jnp.concatenate([it["uw"][:, GDN_D:].astype(BF16), it["q_dec"]], axis=0)
            heads.append((it["uw"][:, :GDN_D], w_q, it["attn"], it["k_dec"], it["g_last"]))
        out.append(heads)
    return out


def _gdn_scan_chunk(heads, state):
    r = [_mm(w_q, s) for (_, w_q, _, _, _), s in zip(heads, state)]
    v_new = [u - r_h[:CHUNK] for (u, _, _, _, _), r_h in zip(heads, r)]
    av = [_mm(attn, vn) for (_, _, attn, _, _), vn in zip(heads, v_new)]
    ds = [_mm_tn(k_dec, vn) for (_, _, _, k_dec, _), vn in zip(heads, v_new)]
    outs = [r_h[CHUNK:] + av_h for r_h, av_h in zip(r, av)]
    new_state = [s * g_last + ds_h for (_, _, _, _, g_last), s, ds_h in zip(heads, state, ds)]
    return outs, new_state


def _head_cols(part, h):
    col = part * GDN_W + h * GDN_D
    return slice(col, col + GDN_D)


def _gdn_meta_kernel(pm_ref, bdm_ref, alog_ref, dtb_ref, s0_ref):
    rows = lax.broadcasted_iota(jnp.int32, (CHUNK, 1), 0)
    row_valid = (rows >= META_PAD).astype(F32)
    qkv = lambda h, part: pm_ref[:, _head_cols(part, h)]
    (heads,) = _gdn_prepare([(qkv, bdm_ref[...], row_valid)], alog_ref, dtb_ref)
    _, state = _gdn_scan_chunk(heads, [jnp.zeros((GDN_D, GDN_D), F32)] * GDN_HEADS)
    for h in range(GDN_HEADS):
        s0_ref[h] = state[h]


def _gdn_kernel(pa_ref, bdx_ref, s0_ref, alog_ref, dtb_ref, nw_ref, y_ref, s_ref, *, chunks_per_tile):
    @pl.when(pl.program_id(1) == 0)
    def _():
        s_ref[...] = s0_ref[...]

    chunks = []
    for c in range(chunks_per_tile):
        rows = slice(c * CHUNK, (c + 1) * CHUNK)
        qkv = lambda h, part, rows=rows: pa_ref[0, rows, _head_cols(part, h)]
        chunks.append((qkv, bdx_ref[0, rows, :], None))

    prepared = _gdn_prepare(chunks, alog_ref, dtb_ref)
    state = [s_ref[h] for h in range(GDN_HEADS)]
    for c in range(chunks_per_tile):
        rows = slice(c * CHUNK, (c + 1) * CHUNK)
        outs, state = _gdn_scan_chunk(prepared[c], state)
        for h in range(GDN_HEADS):
            o = outs[h]
            o = o * lax.rsqrt(jnp.mean(o * o, axis=-1, keepdims=True) + RMS_EPS) * nw_ref[...]
            z = pa_ref[0, rows, _head_cols(3, h)].astype(F32)
            y_ref[0, rows, h * GDN_D:(h + 1) * GDN_D] = (o * z).astype(BF16)
    for h in range(GDN_HEADS):
        s_ref[h] = state[h]


def _gdn(p_x, p_m, bd_x, bd_m, a_log, dt_bias, norm_w, chunks_per_tile=4):
    b, seq, _ = p_x.shape
    tile = chunks_per_tile * CHUNK
    const = lambda *_: (0, 0)
    params = pltpu.CompilerParams(dimension_semantics=("arbitrary", "arbitrary"),
                                  vmem_limit_bytes=VMEM_LIMIT)
    state_shape = (GDN_HEADS, GDN_D, GDN_D)

    s0 = pl.pallas_call(
        _gdn_meta_kernel,
        grid=(1, 1),
        in_specs=[
            pl.BlockSpec((CHUNK, CONV_W), const),
            pl.BlockSpec((CHUNK, BD_W), const),
            pl.BlockSpec((1, GDN_HEADS), const),
            pl.BlockSpec((1, GDN_HEADS), const),
        ],
        out_specs=pl.BlockSpec(state_shape, lambda *_: (0, 0, 0)),
        out_shape=jax.ShapeDtypeStruct(state_shape, F32),
        compiler_params=params,
        name="gdn_meta",
    )(p_m, bd_m, a_log, dt_bias)

    return pl.pallas_call(
        functools.partial(_gdn_kernel, chunks_per_tile=chunks_per_tile),
        grid=(b, seq // tile),
        in_specs=[
            pl.BlockSpec((1, tile, 4 * GDN_W), lambda i, t: (i, t, 0)),
            pl.BlockSpec((1, tile, BD_W), lambda i, t: (i, t, 0)),
            pl.BlockSpec(state_shape, lambda i, t: (0, 0, 0)),
            pl.BlockSpec((1, GDN_HEADS), const),
            pl.BlockSpec((1, GDN_HEADS), const),
            pl.BlockSpec((1, GDN_D), const),
        ],
        out_specs=pl.BlockSpec((1, tile, GDN_W), lambda i, t: (i, t, 0)),
        out_shape=jax.ShapeDtypeStruct((b, seq, GDN_W), BF16),
        scratch_shapes=[pltpu.VMEM(state_shape, F32)],
        compiler_params=params,
        name="gdn",
    )(p_x, bd_x, s0, a_log, dt_bias, norm_w)


def _swa_kernel(q_ref, z_ref, kv_ref, kvm_ref, sinks_ref, y_ref, *, chunks_per_tile):
    t = pl.program_id(1)
    band = (WINDOW_CHUNKS + 1) * CHUNK
    n_keys = N_META + band
    rows = SWA_GROUPS * CHUNK

    qi = lax.broadcasted_iota(jnp.int32, (rows, n_keys), 0)
    kj = lax.broadcasted_iota(jnp.int32, (rows, n_keys), 1)
    q_in_chunk = qi % CHUNK
    group = lax.broadcasted_iota(jnp.int32, (rows, 1), 0) // CHUNK
    is_meta = kj < N_META
    band_chunk = (kj - N_META) // CHUNK
    rel_meta = q_in_chunk + N_META - kj
    rel_band = q_in_chunk - (kj - N_META)

    for c in range(chunks_per_tile):
        r0 = c * CHUNK
        m = t * chunks_per_tile + c
        ws = jnp.maximum(m - WINDOW_CHUNKS, 0)
        k0 = pl.multiple_of(ws * CHUNK, CHUNK)
        dist = jnp.abs(jnp.where(is_meta, rel_meta + m * CHUNK, rel_band + (m - ws) * CHUNK)).astype(F32)
        valid = is_meta | (band_chunk + ws <= m)

        kv_win = kv_ref[0, pl.ds(k0, band), :]
        kv_meta = kvm_ref[META_PAD:CHUNK, :]
        for kvh in range(SWA_KV_HEADS):
            kcol = kvh * SWA_HD
            vcol = SWA_KV_W + kvh * SWA_HD
            k_win = jnp.concatenate([kv_meta[:, kcol:kcol + SWA_HD], kv_win[:, kcol:kcol + SWA_HD]], axis=0)
            v_win = jnp.concatenate([kv_meta[:, vcol:vcol + SWA_HD], kv_win[:, vcol:vcol + SWA_HD]], axis=0)
            q_stack = jnp.concatenate(
                [q_ref[0, r0:r0 + CHUNK, (kvh * SWA_GROUPS + g) * SWA_HD:(kvh * SWA_GROUPS + g + 1) * SWA_HD]
                 for g in range(SWA_GROUPS)], axis=0)
            scores = _mm_nt(q_stack, k_win)
            slope = jnp.zeros((rows, 1), F32)
            sink = jnp.zeros((rows, 1), F32)
            for g in range(SWA_GROUPS):
                head = kvh * SWA_GROUPS + g
                slope = jnp.where(group == g, 2.0 ** (-8.0 * (head + 1) / SWA_HEADS), slope)
                sink = jnp.where(group == g, sinks_ref[0, head], sink)
            scores = jnp.where(valid, scores - slope * dist, -jnp.inf)
            mx = jnp.maximum(jnp.max(scores, axis=-1, keepdims=True), sink)
            e = jnp.exp(scores - mx)
            denom = jnp.sum(e, axis=-1, keepdims=True) + jnp.exp(sink - mx)
            probs = e * (1.0 / denom)
            o = _mm(probs, v_win)
            o = jnp.concatenate([o[g * CHUNK:(g + 1) * CHUNK, :] for g in range(SWA_GROUPS)], axis=1)
            zc = kvh * SWA_GROUPS * SWA_HD
            z = z_ref[0, r0:r0 + CHUNK, zc:zc + SWA_GROUPS * SWA_HD].astype(F32)
            y_ref[0, r0:r0 + CHUNK, zc:zc + SWA_GROUPS * SWA_HD] = (o * z).astype(BF16)


def _swa(p_x, p_m, sinks, chunks_per_tile=4):
    b, seq, _ = p_x.shape
    tile = chunks_per_tile * CHUNK
    return pl.pallas_call(
        functools.partial(_swa_kernel, chunks_per_tile=chunks_per_tile),
        grid=(b, seq // tile),
        in_specs=[
            pl.BlockSpec((1, tile, SWA_W), lambda i, t: (i, t, COL_Q_B // SWA_W)),
            pl.BlockSpec((1, tile, SWA_W), lambda i, t: (i, t, COL_Z_B // SWA_W)),
            pl.BlockSpec((1, seq, 2 * SWA_KV_W), lambda i, t: (i, 0, COL_KV_B // (2 * SWA_KV_W))),
            pl.BlockSpec((CHUNK, 2 * SWA_KV_W), lambda i, t: (0, COL_KV_B // (2 * SWA_KV_W))),
            pl.BlockSpec(memory_space=pltpu.SMEM),
        ],
        out_specs=pl.BlockSpec((1, tile, SWA_W), lambda i, t: (i, t, 0)),
        out_shape=jax.ShapeDtypeStruct((b, seq, SWA_W), BF16),
        compiler_params=pltpu.CompilerParams(
            dimension_semantics=("arbitrary", "arbitrary"), vmem_limit_bytes=VMEM_LIMIT),
        name="swa",
    )(p_x, p_x, p_x, p_m, sinks)


def _out_kernel(x_ref, ya_ref, yb_ref, ga_ref, gb_ref, wa_ref, wb_ref, wo_ref, lnw_ref, lnb_ref, o_ref):
    mixed = (ga_ref[...].astype(F32) * _mm(ya_ref[...], wa_ref[...])
             + gb_ref[...].astype(F32) * _mm(yb_ref[...], wb_ref[...]))
    r = DEEPNORM_ALPHA * x_ref[...] + _mm(mixed, wo_ref[...])
    mu = jnp.mean(r, axis=-1, keepdims=True)
    d = r - mu
    var = jnp.mean(d * d, axis=-1, keepdims=True)
    o_ref[...] = d * lax.rsqrt(var + LN_EPS) * lnw_ref[...] + lnb_ref[...]


def _output(x2d, y_a, y_b, p_x2d, w_a, w_b, w_o, ln_w, ln_b, tm=512):
    m = x2d.shape[0]
    row = lambda i: (i, 0)
    const = lambda i: (0, 0)
    return pl.pallas_call(
        _out_kernel,
        grid=(m // tm,),
        in_specs=[
            pl.BlockSpec((tm, D_MODEL), row),
            pl.BlockSpec((tm, D_MODEL), row),
            pl.BlockSpec((tm, D_MODEL), row),
            pl.BlockSpec((tm, D_MODEL), lambda i: (i, COL_GATE_A // D_MODEL)),
            pl.BlockSpec((tm, D_MODEL), lambda i: (i, COL_GATE_B // D_MODEL)),
            pl.BlockSpec((D_MODEL, D_MODEL), const),
            pl.BlockSpec((D_MODEL, D_MODEL), const),
            pl.BlockSpec((D_MODEL, D_MODEL), const),
            pl.BlockSpec((1, D_MODEL), const),
            pl.BlockSpec((1, D_MODEL), const),
        ],
        out_specs=pl.BlockSpec((tm, D_MODEL), row),
        out_shape=jax.ShapeDtypeStruct((m, D_MODEL), F32),
        compiler_params=pltpu.CompilerParams(
            dimension_semantics=("arbitrary",), vmem_limit_bytes=VMEM_LIMIT),
        name="merge_out",
    )(x2d, y_a, y_b, p_x2d, p_x2d, w_a, w_b, w_o, ln_w, ln_b)


def _one_layer(x, meta_tokens, w_in, b_gate, conv_w, a_log, dt_bias, gdn_norm_w, sinks,
               w_proj_a, w_proj_b, w_out, ln_w, ln_b):
    b, seq, _ = x.shape
    o_bd = 4 * GDN_W
    o_qb = o_bd + 2 * GDN_HEADS
    o_kb = o_qb + SWA_W
    o_zb = o_kb + 2 * SWA_KV_W
    w_main = jnp.concatenate(
        [w_in[:, :o_bd], w_in[:, o_qb:o_kb], w_in[:, o_zb:], w_in[:, o_kb:o_zb]], axis=1).astype(BF16)
    w_bd = jnp.pad(w_in[:, o_bd:o_qb], ((0, 0), (0, BD_W - 2 * GDN_HEADS))).astype(BF16)
    b_gate = b_gate.reshape(1, 2 * D_MODEL)

    x2d = x.reshape(b * seq, D_MODEL)
    meta_chunk = jnp.concatenate([jnp.zeros((META_PAD, D_MODEL), x.dtype), meta_tokens.astype(x.dtype)], axis=0)

    no_halo = jnp.zeros((CONV_HALO, CONV_W), F32)
    p_m, bd_m, meta_tail = _project(meta_chunk, w_main, w_bd, conv_w, b_gate, no_halo,
                                    tm=CHUNK, tiles_per_seq=1)
    p_x2d, bd_x2d, _ = _project(x2d, w_main, w_bd, conv_w, b_gate, meta_tail,
                                tm=PROJ_TM, tiles_per_seq=seq // PROJ_TM)
    p_x = p_x2d.reshape(b, seq, PROJ_W)
    bd_x = bd_x2d.reshape(b, seq, BD_W)

    y_a = _gdn(p_x, p_m, bd_x, bd_m, a_log.reshape(1, GDN_HEADS), dt_bias.reshape(1, GDN_HEADS),
               gdn_norm_w.reshape(1, GDN_D))
    y_b = _swa(p_x, p_m, sinks.reshape(1, SWA_HEADS))

    out = _output(x2d, y_a.reshape(b * seq, GDN_W), y_b.reshape(b * seq, SWA_W), p_x2d,
                  w_proj_a.astype(BF16), w_proj_b.astype(BF16), w_out.astype(BF16),
                  ln_w.reshape(1, D_MODEL), ln_b.reshape(1, D_MODEL))
    return out.reshape(b, seq, D_MODEL)


def kernel(x, meta_tokens, w_in, b_gate, conv_w, a_log, dt_bias, gdn_norm_w, sinks,
           w_proj_a, w_proj_b, w_out, ln_w, ln_b):
    depth = w_in.shape[0]
    assert depth == 1, "meta tokens are projected once; deeper stacks need per-layer meta rows"
    return _one_layer(x, meta_tokens, w_in[0], b_gate[0], conv_w[0], a_log[0], dt_bias[0],
                      gdn_norm_w[0], sinks[0], w_proj_a[0], w_proj_b[0], w_out[0], ln_w[0], ln_b[0])
```

```python
import functools

import jax
import jax.numpy as jnp
from jax import lax
from jax.experimental import pallas as pl
from jax.experimental.pallas import tpu as pltpu

F32 = jnp.float32
BF16 = jnp.bfloat16

D_MODEL = 1024
CHUNK = 64
N_META = 16
META_PAD = CHUNK - N_META
GDN_HEADS = 8
GDN_D = 128
GDN_CONV = 4
GDN_W = GDN_HEADS * GDN_D
SWA_HEADS = 16
SWA_KV_HEADS = 4
SWA_GROUPS = SWA_HEADS // SWA_KV_HEADS
SWA_HD = 64
SWA_W = SWA_HEADS * SWA_HD
SWA_KV_W = SWA_KV_HEADS * SWA_HD
WINDOW_CHUNKS = 2
DEEPNORM_ALPHA = 2.0 ** 0.25
LN_EPS = 1e-5
RMS_EPS = 1e-6
L2_EPS = 1e-6
LOG2E = 1.4426950408889634

COL_Q_A = 0
COL_K_A = GDN_W
COL_V_A = 2 * GDN_W
COL_Z_A = 3 * GDN_W
COL_Q_B = 4 * GDN_W
COL_Z_B = COL_Q_B + SWA_W
COL_GATE_A = COL_Z_B + SWA_W
COL_GATE_B = COL_GATE_A + D_MODEL
COL_KV_B = COL_GATE_B + D_MODEL
PROJ_W = COL_KV_B + 2 * SWA_KV_W
CONV_W = 3 * GDN_W
BD_W = 128

PROJ_TN = 256
PROJ_TM = 512
CONV_HALO = 8
VMEM_LIMIT = 48 * 1024 * 1024
PROJ_VMEM_LIMIT = 56 * 1024 * 1024


def _mm(a, b):
    return jnp.dot(a.astype(BF16), b.astype(BF16), preferred_element_type=F32)


def _mm_nt(a, b):
    return lax.dot_general(a.astype(BF16), b.astype(BF16), (((1,), (1,)), ((), ())),
                           preferred_element_type=F32)


def _mm_tn(a, b):
    return lax.dot_general(a.astype(BF16), b.astype(BF16), (((0,), (0,)), ((), ())),
                           preferred_element_type=F32)


def _mm_f32(a, b):
    return jnp.dot(a, b, preferred_element_type=F32, precision=lax.Precision.HIGHEST)


def _mm_tn_f32(a, b):
    return lax.dot_general(a, b, (((0,), (0,)), ((), ())), preferred_element_type=F32,
                           precision=lax.Precision.HIGHEST)


def _sigmoid(x):
    return 1.0 / (1.0 + jnp.exp2(x * -LOG2E))


SUBLANES = 8


def _shift_rows(a, k):
    rows, width = a.shape
    tiles = a.reshape(rows // SUBLANES, SUBLANES, width)
    rolled = pltpu.roll(tiles, k, axis=1)
    above = jnp.concatenate([rolled[:1], rolled[:-1]], axis=0)
    sub = lax.broadcasted_iota(jnp.int32, tiles.shape, 1)
    return jnp.where(sub < k, above, rolled).reshape(rows, width)


def _silu(x):
    return x * _sigmoid(x)


def _softplus(x):
    return jnp.maximum(x, 0.0) + jnp.log(1.0 + jnp.exp(-jnp.abs(x)))


def _proj_kernel(x_ref, w_ref, wbd_ref, convw_ref, bg_ref, halo0_ref, p_ref, bd_ref, tail_ref,
                 carry_ref, xb_ref, *, tm, tiles_per_seq):
    first_of_seq = (pl.program_id(0) % tiles_per_seq) == 0
    xb_ref[...] = x_ref[...].astype(BF16)
    bd_ref[...] = jnp.dot(xb_ref[...], wbd_ref[...], preferred_element_type=F32)

    n_tiles = PROJ_W // PROJ_TN
    heavy = list(range(CONV_W // PROJ_TN))
    light = list(range(CONV_W // PROJ_TN, n_tiles))
    order = []
    while heavy or light:
        order += heavy[:1] + light[:2]
        heavy, light = heavy[1:], light[2:]

    def matmul(j):
        return jnp.dot(xb_ref[...], w_ref[:, j * PROJ_TN:(j + 1) * PROJ_TN], preferred_element_type=F32)

    def epilogue(j, acc):
        c0 = j * PROJ_TN
        cols = slice(c0, c0 + PROJ_TN)
        if c0 < CONV_W:
            halo = jnp.where(first_of_seq, halo0_ref[:, cols], carry_ref[:, cols])
            tail = acc[tm - CONV_HALO:tm, :]
            carry_ref[:, cols] = tail
            tail_ref[:, cols] = tail
            raw = jnp.concatenate([halo, acc], axis=0)
            w0, w1, w2, w3 = (convw_ref[tap:tap + 1, cols] for tap in range(GDN_CONV))
            back1 = _shift_rows(raw, 1)
            u = raw * w1 + back1 * w0
            y = (acc * w3 + back1[CONV_HALO:, :] * w2) + _shift_rows(u, 2)[CONV_HALO:, :]
            y = _silu(y)
            if c0 < COL_V_A:
                post = GDN_D ** -0.5 if c0 < COL_K_A else 1.0
                groups = []
                for g in range(PROJ_TN // GDN_D):
                    yg = y[:, g * GDN_D:(g + 1) * GDN_D]
                    ss = jnp.sum(yg * yg, axis=-1, keepdims=True)
                    groups.append(yg * (lax.rsqrt(ss + L2_EPS) * post))
                y = jnp.concatenate(groups, axis=1)
        elif c0 < COL_Q_B or COL_Z_B <= c0 < COL_GATE_A:
            y = _silu(acc)
        elif c0 < COL_Z_B:
            y = acc * (LOG2E * SWA_HD ** -0.5)
        elif c0 < COL_KV_B:
            y = _sigmoid(acc + bg_ref[:, c0 - COL_GATE_A:c0 - COL_GATE_A + PROJ_TN])
        else:
            y = acc
        p_ref[:, cols] = y.astype(BF16)

    acc = matmul(order[0])
    for i, j in enumerate(order):
        nxt = matmul(order[i + 1]) if i + 1 < len(order) else None
        epilogue(j, acc)
        acc = nxt


def _project(x2d, w_main, w_bd, conv_w, b_gate, halo0, tm, tiles_per_seq):
    m = x2d.shape[0]
    n_tiles = m // tm
    const = lambda i: (0, 0)
    resident = dict(pipeline_mode=pl.Buffered(1))
    return pl.pallas_call(
        functools.partial(_proj_kernel, tm=tm, tiles_per_seq=tiles_per_seq),
        grid=(n_tiles,),
        in_specs=[
            pl.BlockSpec((tm, D_MODEL), lambda i: (i, 0)),
            pl.BlockSpec((D_MODEL, PROJ_W), const, **resident),
            pl.BlockSpec((D_MODEL, BD_W), const, **resident),
            pl.BlockSpec((GDN_CONV, CONV_W), const, **resident),
            pl.BlockSpec((1, 2 * D_MODEL), const, **resident),
            pl.BlockSpec((CONV_HALO, CONV_W), const, **resident),
        ],
        out_specs=[
            pl.BlockSpec((tm, PROJ_W), lambda i: (i, 0)),
            pl.BlockSpec((tm, BD_W), lambda i: (i, 0)),
            pl.BlockSpec((CONV_HALO, CONV_W), lambda i: (i, 0)),
        ],
        out_shape=[
            jax.ShapeDtypeStruct((m, PROJ_W), BF16),
            jax.ShapeDtypeStruct((m, BD_W), F32),
            jax.ShapeDtypeStruct((n_tiles * CONV_HALO, CONV_W), F32),
        ],
        scratch_shapes=[pltpu.VMEM((CONV_HALO, CONV_W), F32), pltpu.VMEM((tm, D_MODEL), BF16)],
        compiler_params=pltpu.CompilerParams(
            dimension_semantics=("arbitrary",), vmem_limit_bytes=PROJ_VMEM_LIMIT),
        name="proj",
    )(x2d, w_main, w_bd, conv_w, b_gate, halo0)


def _gdn_prepare(chunks, alog_ref, dtb_ref):
    ri = lax.broadcasted_iota(jnp.int32, (CHUNK, CHUNK), 0)
    ci = lax.broadcasted_iota(jnp.int32, (CHUNK, CHUNK), 1)
    tri = ri >= ci
    strict = ri > ci
    lower_ones = tri.astype(F32)
    upper_ones = (ri <= ci).astype(F32)
    eye = (ri == ci).astype(F32)
    neg_alpha = -jnp.exp(alog_ref[...])

    gates = []
    for qkv, bd, row_valid in chunks:
        beta = _sigmoid(bd[:, 0:GDN_HEADS])
        g = neg_alpha * _softplus(bd[:, GDN_HEADS:2 * GDN_HEADS] + dtb_ref[...])
        if row_valid is not None:
            g = g * row_valid
        gates.append((beta, _mm_f32(lower_ones, g), _mm_tn_f32(g, upper_ones)))

    inst = []
    for (qkv, bd, row_valid), (beta, g_cum, g_cum_t) in zip(chunks, gates):
        for h in range(GDN_HEADS):
            q = qkv(h, 0).astype(F32)
            k_bf = qkv(h, 1)
            k = k_bf.astype(F32)
            v = qkv(h, 2).astype(F32)
            b_col = beta[:, h:h + 1]
            gc_col = g_cum[:, h:h + 1]
            gc_row = g_cum_t[h:h + 1, :]
            gc_last = g_cum_t[h:h + 1, CHUNK - 1:CHUNK]
            e_col = jnp.exp(gc_col)
            k_beta = k * b_col
            inst.append(dict(
                k=k_bf,
                kb_q=jnp.concatenate([k_beta, q], axis=0).astype(BF16),
                rhs=jnp.concatenate([v * b_col, k_beta * e_col], axis=1).astype(BF16),
                q_dec=(q * e_col).astype(BF16),
                k_dec=(k * jnp.exp(gc_last - gc_col)).astype(BF16),
                decay=jnp.where(tri, jnp.exp(gc_col - gc_row), 0.0),
                g_last=jnp.exp(gc_last)))

    for it in inst:
        it["kq"] = _mm_nt(it["kb_q"], it["k"])
    for it in inst:
        a = jnp.where(strict, it["kq"][:CHUNK] * it["decay"], 0.0)
        it["a"] = a
        it["attn"] = (it["kq"][CHUNK:] * it["decay"]).astype(BF16)
        it["t"] = eye - jnp.where((ri // 2) == (ci // 2), a, 0.0)

    s = 2
    while s < CHUNK:
        off = ((ri // (2 * s)) == (ci // (2 * s))) & ((ri // s) != (ci // s))
        for it in inst:
            it["x"] = _mm(it["t"], jnp.where(off, it["a"], 0.0))
        for it in inst:
            it["x"] = _mm(it["x"], it["t"])
        for it in inst:
            it["t"] = it["t"] - it["x"]
        s *= 2

    for it in inst:
        it["uw"] = _mm(it["t"], it["rhs"])

    out = []
    for c in range(len(chunks)):
        heads = []
        for h in range(GDN_HEADS):
            it = inst[c * GDN_HEADS + h]
            w_q = jnp.concatenate([it["uw"][:, GDN_D:].astype(BF16), it["q_dec"]], axis=0)
            heads.append((it["uw"][:, :GDN_D], w_q, it["attn"], it["k_dec"], it["g_last"]))
        out.append(heads)
    return out


def _gdn_scan_chunk(heads, state):
    r = [_mm(w_q, s) for (_, w_q, _, _, _), s in zip(heads, state)]
    v_new = [u - r_h[:CHUNK] for (u, _, _, _, _), r_h in zip(heads, r)]
    av = [_mm(attn, vn) for (_, _, attn, _, _), vn in zip(heads, v_new)]
    ds = [_mm_tn(k_dec, vn) for (_, _, _, k_dec, _), vn in zip(heads, v_new)]
    outs = [r_h[CHUNK:] + av_h for r_h, av_h in zip(r, av)]
    new_state = [s * g_last + ds_h for (_, _, _, _, g_last), s, ds_h in zip(heads, state, ds)]
    return outs, new_state


def _head_cols(part, h):
    col = part * GDN_W + h * GDN_D
    return slice(col, col + GDN_D)


def _gdn_meta_kernel(pm_ref, bdm_ref, alog_ref, dtb_ref, s0_ref):
    rows = lax.broadcasted_iota(jnp.int32, (CHUNK, 1), 0)
    row_valid = (rows >= META_PAD).astype(F32)
    qkv = lambda h, part: pm_ref[:, _head_cols(part, h)]
    (heads,) = _gdn_prepare([(qkv, bdm_ref[...], row_valid)], alog_ref, dtb_ref)
    _, state = _gdn_scan_chunk(heads, [jnp.zeros((GDN_D, GDN_D), F32)] * GDN_HEADS)
    for h in range(GDN_HEADS):
        s0_ref[h] = state[h]


def _gdn_kernel(pa_ref, bdx_ref, s0_ref, alog_ref, dtb_ref, nw_ref, y_ref, s_ref, *, chunks_per_tile):
    @pl.when(pl.program_id(1) == 0)
    def _():
        s_ref[...] = s0_ref[...]

    chunks = []
    for c in range(chunks_per_tile):
        rows = slice(c * CHUNK, (c + 1) * CHUNK)
        qkv = lambda h, part, rows=rows: pa_ref[0, rows, _head_cols(part, h)]
        chunks.append((qkv, bdx_ref[0, rows, :], None))

    prepared = _gdn_prepare(chunks, alog_ref, dtb_ref)
    state = [s_ref[h] for h in range(GDN_HEADS)]
    for c in range(chunks_per_tile):
        rows = slice(c * CHUNK, (c + 1) * CHUNK)
        outs, state = _gdn_scan_chunk(prepared[c], state)
        for h in range(GDN_HEADS):
            o = outs[h]
            o = o * lax.rsqrt(jnp.mean(o * o, axis=-1, keepdims=True) + RMS_EPS) * nw_ref[...]
            z = pa_ref[0, rows, _head_cols(3, h)].astype(F32)
            y_ref[0, rows, h * GDN_D:(h + 1) * GDN_D] = (o * z).astype(BF16)
    for h in range(GDN_HEADS):
        s_ref[h] = state[h]


def _gdn(p_x, p_m, bd_x, bd_m, a_log, dt_bias, norm_w, chunks_per_tile=4):
    b, seq, _ = p_x.shape
    tile = chunks_per_tile * CHUNK
    const = lambda *_: (0, 0)
    params = pltpu.CompilerParams(dimension_semantics=("arbitrary", "arbitrary"),
                                  vmem_limit_bytes=VMEM_LIMIT)
    state_shape = (GDN_HEADS, GDN_D, GDN_D)

    s0 = pl.pallas_call(
        _gdn_meta_kernel,
        grid=(1, 1),
        in_specs=[
            pl.BlockSpec((CHUNK, CONV_W), const),
            pl.BlockSpec((CHUNK, BD_W), const),
            pl.BlockSpec((1, GDN_HEADS), const),
            pl.BlockSpec((1, GDN_HEADS), const),
        ],
        out_specs=pl.BlockSpec(state_shape, lambda *_: (0, 0, 0)),
        out_shape=jax.ShapeDtypeStruct(state_shape, F32),
        compiler_params=params,
        name="gdn_meta",
    )(p_m, bd_m, a_log, dt_bias)

    return pl.pallas_call(
        functools.partial(_gdn_kernel, chunks_per_tile=chunks_per_tile),
        grid=(b, seq // tile),
        in_specs=[
            pl.BlockSpec((1, tile, 4 * GDN_W), lambda i, t: (i, t, 0)),
            pl.BlockSpec((1, tile, BD_W), lambda i, t: (i, t, 0)),
            pl.BlockSpec(state_shape, lambda i, t: (0, 0, 0)),
            pl.BlockSpec((1, GDN_HEADS), const),
            pl.BlockSpec((1, GDN_HEADS), const),
            pl.BlockSpec((1, GDN_D), const),
        ],
        out_specs=pl.BlockSpec((1, tile, GDN_W), lambda i, t: (i, t, 0)),
        out_shape=jax.ShapeDtypeStruct((b, seq, GDN_W), BF16),
        scratch_shapes=[pltpu.VMEM(state_shape, F32)],
        compiler_params=params,
        name="gdn",
    )(p_x, bd_x, s0, a_log, dt_bias, norm_w)


SWA_BAND = (WINDOW_CHUNKS + 1) * CHUNK
SWA_KEYS = SWA_BAND + N_META
SWA_ROWS = SWA_GROUPS * CHUNK
LANES = 128
CHUNK_SHIFT = CHUNK.bit_length() - 1
SWA_AHEAD = 4


def _swa_tables(bias_ref, mterm_ref):
    qi = lax.broadcasted_iota(jnp.int32, (SWA_ROWS, SWA_KEYS), 0)
    kj = lax.broadcasted_iota(jnp.int32, (SWA_ROWS, SWA_KEYS), 1)
    q_in_chunk = jnp.bitwise_and(qi, CHUNK - 1)
    group = lax.shift_right_logical(qi, CHUNK_SHIFT)
    key_chunk = lax.shift_right_logical(kj, CHUNK_SHIFT)
    key_row = jnp.bitwise_and(kj, CHUNK - 1)
    is_meta = kj >= SWA_BAND
    for kvh in range(SWA_KV_HEADS):
        slope = jnp.zeros((SWA_ROWS, SWA_KEYS), F32)
        for g in range(SWA_GROUPS):
            head = kvh * SWA_GROUPS + g
            slope = jnp.where(group == g, LOG2E * 2.0 ** (-8.0 * (head + 1) / SWA_HEADS), slope)
        meta_bias = -slope * (q_in_chunk + N_META - (kj - SWA_BAND)).astype(F32)
        for e in range(WINDOW_CHUNKS + 1):
            d = (e - key_chunk) * CHUNK + q_in_chunk - key_row
            band_bias = jnp.where(key_chunk <= e, -slope * jnp.abs(d).astype(F32), -jnp.inf)
            bias_ref[e, kvh] = jnp.where(is_meta, meta_bias, band_bias)
        mterm_ref[kvh] = jnp.where(is_meta, slope * CHUNK, 0.0)[:, LANES:]


def _swa_kernel(q_ref, z_ref, kv_ref, kvm_ref, sinks_ref, y_ref, bias_ref, mterm_ref, *, chunks_per_tile):
    t = pl.program_id(1)

    @pl.when((pl.program_id(0) == 0) & (t == 0))
    def _():
        _swa_tables(bias_ref, mterm_ref)

    group = lax.shift_right_logical(lax.broadcasted_iota(jnp.int32, (SWA_ROWS, 1), 0), CHUNK_SHIFT)
    sink_cols = []
    for kvh in range(SWA_KV_HEADS):
        sink = jnp.zeros((SWA_ROWS, 1), F32)
        for g in range(SWA_GROUPS):
            sink = jnp.where(group == g, sinks_ref[0, kvh * SWA_GROUPS + g] * LOG2E, sink)
        sink_cols.append(sink)
    kv_meta = kvm_ref[META_PAD:CHUNK, :]

    def scores(c, kvh):
        m = t * chunks_per_tile + c
        k0 = pl.multiple_of(jnp.maximum(m - WINDOW_CHUNKS, 0) * CHUNK, CHUNK)
        kv_win = kv_ref[0, pl.ds(k0, SWA_BAND), :]
        kcol = kvh * SWA_HD
        vcol = SWA_KV_W + kvh * SWA_HD
        k_win = jnp.concatenate([kv_win[:, kcol:kcol + SWA_HD], kv_meta[:, kcol:kcol + SWA_HD]], axis=0)
        v_win = jnp.concatenate([kv_win[:, vcol:vcol + SWA_HD], kv_meta[:, vcol:vcol + SWA_HD]], axis=0)
        rows = slice(c * CHUNK, (c + 1) * CHUNK)
        q_stack = jnp.concatenate(
            [q_ref[0, rows, (kvh * SWA_GROUPS + g) * SWA_HD:(kvh * SWA_GROUPS + g + 1) * SWA_HD]
             for g in range(SWA_GROUPS)], axis=0)
        return _mm_nt(q_stack, k_win), v_win, m

    def softmax(kvh, s, m):
        s = s + bias_ref[jnp.minimum(m, WINDOW_CHUNKS), kvh]
        s = jnp.concatenate([s[:, :LANES], s[:, LANES:] - mterm_ref[kvh] * m.astype(F32)], axis=1)
        sink = sink_cols[kvh]
        mx = jnp.maximum(jnp.max(s, axis=-1, keepdims=True), sink)
        return jnp.exp2(s - mx).astype(BF16), jnp.exp2(sink - mx)

    def store(c, kvh, o, key_sum, sink_term):
        o = o * (1.0 / (key_sum[:, :SWA_HD] + sink_term))
        o = jnp.concatenate([o[g * CHUNK:(g + 1) * CHUNK, :] for g in range(SWA_GROUPS)], axis=1)
        rows = slice(c * CHUNK, (c + 1) * CHUNK)
        cols = slice(kvh * SWA_GROUPS * SWA_HD, (kvh + 1) * SWA_GROUPS * SWA_HD)
        y_ref[0, rows, cols] = (o * z_ref[0, rows, cols].astype(F32)).astype(BF16)

    order = [(c, kvh) for c in range(chunks_per_tile) for kvh in range(SWA_KV_HEADS)]
    scored = [scores(c, kvh) for c, kvh in order]
    probs = [softmax(kvh, s, m) for (c, kvh), (s, _, m) in zip(order, scored)]
    outs = [_mm(e, v_win) for (e, _), (_, v_win, _) in zip(probs, scored)]
    ones = jnp.ones((SWA_KEYS, LANES), BF16)
    sums = [_mm(e, ones) for e, _ in probs]
    for (c, kvh), o, key_sum, (_, sink_term) in zip(order, outs, sums, probs):
        store(c, kvh, o, key_sum, sink_term)


def _swa(p_x, p_m, sinks, chunks_per_tile=4):
    b, seq, _ = p_x.shape
    tile = chunks_per_tile * CHUNK
    return pl.pallas_call(
        functools.partial(_swa_kernel, chunks_per_tile=chunks_per_tile),
        grid=(b, seq // tile),
        in_specs=[
            pl.BlockSpec((1, tile, SWA_W), lambda i, t: (i, t, COL_Q_B // SWA_W)),
            pl.BlockSpec((1, tile, SWA_W), lambda i, t: (i, t, COL_Z_B // SWA_W)),
            pl.BlockSpec((1, seq, 2 * SWA_KV_W), lambda i, t: (i, 0, COL_KV_B // (2 * SWA_KV_W))),
            pl.BlockSpec((CHUNK, 2 * SWA_KV_W), lambda i, t: (0, COL_KV_B // (2 * SWA_KV_W))),
            pl.BlockSpec(memory_space=pltpu.SMEM),
        ],
        out_specs=pl.BlockSpec((1, tile, SWA_W), lambda i, t: (i, t, 0)),
        out_shape=jax.ShapeDtypeStruct((b, seq, SWA_W), BF16),
        scratch_shapes=[
            pltpu.VMEM((WINDOW_CHUNKS + 1, SWA_KV_HEADS, SWA_ROWS, SWA_KEYS), F32),
            pltpu.VMEM((SWA_KV_HEADS, SWA_ROWS, SWA_KEYS - LANES), F32),
        ],
        compiler_params=pltpu.CompilerParams(
            dimension_semantics=("arbitrary", "arbitrary"), vmem_limit_bytes=VMEM_LIMIT),
        name="swa",
    )(p_x, p_x, p_x, p_m, sinks)


def _out_kernel(x_ref, ya_ref, yb_ref, ga_ref, gb_ref, wa_ref, wb_ref, wo_ref, lnw_ref, lnb_ref, o_ref):
    mixed = (ga_ref[...].astype(F32) * _mm(ya_ref[...], wa_ref[...])
             + gb_ref[...].astype(F32) * _mm(yb_ref[...], wb_ref[...]))
    r = DEEPNORM_ALPHA * x_ref[...] + _mm(mixed, wo_ref[...])
    mu = jnp.mean(r, axis=-1, keepdims=True)
    d = r - mu
    var = jnp.mean(d * d, axis=-1, keepdims=True)
    o_ref[...] = d * lax.rsqrt(var + LN_EPS) * lnw_ref[...] + lnb_ref[...]


def _output(x2d, y_a, y_b, p_x2d, w_a, w_b, w_o, ln_w, ln_b, tm=512):
    m = x2d.shape[0]
    row = lambda i: (i, 0)
    const = lambda i: (0, 0)
    return pl.pallas_call(
        _out_kernel,
        grid=(m // tm,),
        in_specs=[
            pl.BlockSpec((tm, D_MODEL), row),
            pl.BlockSpec((tm, D_MODEL), row),
            pl.BlockSpec((tm, D_MODEL), row),
            pl.BlockSpec((tm, D_MODEL), lambda i: (i, COL_GATE_A // D_MODEL)),
            pl.BlockSpec((tm, D_MODEL), lambda i: (i, COL_GATE_B // D_MODEL)),
            pl.BlockSpec((D_MODEL, D_MODEL), const),
            pl.BlockSpec((D_MODEL, D_MODEL), const),
            pl.BlockSpec((D_MODEL, D_MODEL), const),
            pl.BlockSpec((1, D_MODEL), const),
            pl.BlockSpec((1, D_MODEL), const),
        ],
        out_specs=pl.BlockSpec((tm, D_MODEL), row),
        out_shape=jax.ShapeDtypeStruct((m, D_MODEL), F32),
        compiler_params=pltpu.CompilerParams(
            dimension_semantics=("arbitrary",), vmem_limit_bytes=VMEM_LIMIT),
        name="merge_out",
    )(x2d, y_a, y_b, p_x2d, p_x2d, w_a, w_b, w_o, ln_w, ln_b)


def _one_layer(x, meta_tokens, w_in, b_gate, conv_w, a_log, dt_bias, gdn_norm_w, sinks,
               w_proj_a, w_proj_b, w_out, ln_w, ln_b):
    b, seq, _ = x.shape
    o_bd = 4 * GDN_W
    o_qb = o_bd + 2 * GDN_HEADS
    o_kb = o_qb + SWA_W
    o_zb = o_kb + 2 * SWA_KV_W
    w_main = jnp.concatenate(
        [w_in[:, :o_bd], w_in[:, o_qb:o_kb], w_in[:, o_zb:], w_in[:, o_kb:o_zb]], axis=1).astype(BF16)
    w_bd = jnp.pad(w_in[:, o_bd:o_qb], ((0, 0), (0, BD_W - 2 * GDN_HEADS))).astype(BF16)
    b_gate = b_gate.reshape(1, 2 * D_MODEL)

    x2d = x.reshape(b * seq, D_MODEL)
    meta_chunk = jnp.concatenate([jnp.zeros((META_PAD, D_MODEL), x.dtype), meta_tokens.astype(x.dtype)], axis=0)

    no_halo = jnp.zeros((CONV_HALO, CONV_W), F32)
    p_m, bd_m, meta_tail = _project(meta_chunk, w_main, w_bd, conv_w, b_gate, no_halo,
                                    tm=CHUNK, tiles_per_seq=1)
    p_x2d, bd_x2d, _ = _project(x2d, w_main, w_bd, conv_w, b_gate, meta_tail,
                                tm=PROJ_TM, tiles_per_seq=seq // PROJ_TM)
    p_x = p_x2d.reshape(b, seq, PROJ_W)
    bd_x = bd_x2d.reshape(b, seq, BD_W)

    y_a = _gdn(p_x, p_m, bd_x, bd_m, a_log.reshape(1, GDN_HEADS), dt_bias.reshape(1, GDN_HEADS),
               gdn_norm_w.reshape(1, GDN_D))
    y_b = _swa(p_x, p_m, sinks.reshape(1, SWA_HEADS))

    out = _output(x2d, y_a.reshape(b * seq, GDN_W), y_b.reshape(b * seq, SWA_W), p_x2d,
                  w_proj_a.astype(BF16), w_proj_b.astype(BF16), w_out.astype(BF16),
                  ln_w.reshape(1, D_MODEL), ln_b.reshape(1, D_MODEL))
    return out.reshape(b, seq, D_MODEL)


def kernel(x, meta_tokens, w_in, b_gate, conv_w, a_log, dt_bias, gdn_norm_w, sinks,
           w_proj_a, w_proj_b, w_out, ln_w, ln_b):
    depth = w_in.shape[0]
    assert depth == 1, "meta tokens are projected once; deeper stacks need per-layer meta rows"
    return _one_layer(x, meta_tokens, w_in[0], b_gate[0], conv_w[0], a_log[0], dt_bias[0],
                      gdn_norm_w[0], sinks[0], w_proj_a[0], w_proj_b[0], w_out[0], ln_w[0], ln_b[0])
```

```python
import functools

import jax
import jax.numpy as jnp
from jax import lax
from jax.experimental import pallas as pl
from jax.experimental.pallas import tpu as pltpu

F32 = jnp.float32
BF16 = jnp.bfloat16

D_MODEL = 1024
CHUNK = 64
N_META = 16
META_PAD = CHUNK - N_META
GDN_HEADS = 8
GDN_D = 128
GDN_CONV = 4
GDN_W = GDN_HEADS * GDN_D
SWA_HEADS = 16
SWA_KV_HEADS = 4
SWA_GROUPS = SWA_HEADS // SWA_KV_HEADS
SWA_HD = 64
SWA_W = SWA_HEADS * SWA_HD
SWA_KV_W = SWA_KV_HEADS * SWA_HD
WINDOW_CHUNKS = 2
DEEPNORM_ALPHA = 2.0 ** 0.25
LN_EPS = 1e-5
RMS_EPS = 1e-6
L2_EPS = 1e-6
LOG2E = 1.4426950408889634

COL_Q_A = 0
COL_K_A = GDN_W
COL_V_A = 2 * GDN_W
COL_Z_A = 3 * GDN_W
COL_Q_B = 4 * GDN_W
COL_Z_B = COL_Q_B + SWA_W
COL_GATE_A = COL_Z_B + SWA_W
COL_GATE_B = COL_GATE_A + D_MODEL
COL_KV_B = COL_GATE_B + D_MODEL
PROJ_W = COL_KV_B + 2 * SWA_KV_W
CONV_W = 3 * GDN_W
BD_W = 128

PROJ_TN = 256
PROJ_TM = 512
CONV_HALO = 8
VMEM_LIMIT = 48 * 1024 * 1024
PROJ_VMEM_LIMIT = 56 * 1024 * 1024


def _mm(a, b):
    return jnp.dot(a.astype(BF16), b.astype(BF16), preferred_element_type=F32)


def _mm_nt(a, b):
    return lax.dot_general(a.astype(BF16), b.astype(BF16), (((1,), (1,)), ((), ())),
                           preferred_element_type=F32)


def _mm_tn(a, b):
    return lax.dot_general(a.astype(BF16), b.astype(BF16), (((0,), (0,)), ((), ())),
                           preferred_element_type=F32)


def _mm_f32(a, b):
    return jnp.dot(a, b, preferred_element_type=F32, precision=lax.Precision.HIGHEST)


def _mm_tn_f32(a, b):
    return lax.dot_general(a, b, (((0,), (0,)), ((), ())), preferred_element_type=F32,
                           precision=lax.Precision.HIGHEST)


def _sigmoid(x):
    return 1.0 / (1.0 + jnp.exp2(x * -LOG2E))


SUBLANES = 8


def _shift_rows(a, k):
    rows, width = a.shape
    tiles = a.reshape(rows // SUBLANES, SUBLANES, width)
    rolled = pltpu.roll(tiles, k, axis=1)
    above = jnp.concatenate([rolled[:1], rolled[:-1]], axis=0)
    sub = lax.broadcasted_iota(jnp.int32, tiles.shape, 1)
    return jnp.where(sub < k, above, rolled).reshape(rows, width)


def _silu(x):
    return x * _sigmoid(x)


def _softplus(x):
    return jnp.maximum(x, 0.0) + jnp.log(1.0 + jnp.exp(-jnp.abs(x)))


def _proj_kernel(x_ref, w_ref, wbd_ref, convw_ref, bg_ref, halo0_ref, p_ref, bd_ref, tail_ref,
                 carry_ref, xb_ref, *, tm, tiles_per_seq):
    first_of_seq = (pl.program_id(0) % tiles_per_seq) == 0
    xb_ref[...] = x_ref[...].astype(BF16)
    bd_ref[...] = jnp.dot(xb_ref[...], wbd_ref[...], preferred_element_type=F32)

    n_tiles = PROJ_W // PROJ_TN
    heavy = list(range(CONV_W // PROJ_TN))
    light = list(range(CONV_W // PROJ_TN, n_tiles))
    order = []
    while heavy or light:
        order += heavy[:1] + light[:2]
        heavy, light = heavy[1:], light[2:]

    def matmul(j):
        return jnp.dot(xb_ref[...], w_ref[:, j * PROJ_TN:(j + 1) * PROJ_TN], preferred_element_type=F32)

    def epilogue(j, acc):
        c0 = j * PROJ_TN
        cols = slice(c0, c0 + PROJ_TN)
        if c0 < CONV_W:
            halo = jnp.where(first_of_seq, halo0_ref[:, cols], carry_ref[:, cols])
            tail = acc[tm - CONV_HALO:tm, :]
            carry_ref[:, cols] = tail
            tail_ref[:, cols] = tail
            raw = jnp.concatenate([halo, acc], axis=0)
            w0, w1, w2, w3 = (convw_ref[tap:tap + 1, cols] for tap in range(GDN_CONV))
            back1 = _shift_rows(raw, 1)
            u = raw * w1 + back1 * w0
            y = (acc * w3 + back1[CONV_HALO:, :] * w2) + _shift_rows(u, 2)[CONV_HALO:, :]
            y = _silu(y)
            if c0 < COL_V_A:
                post = GDN_D ** -0.5 if c0 < COL_K_A else 1.0
                groups = []
                for g in range(PROJ_TN // GDN_D):
                    yg = y[:, g * GDN_D:(g + 1) * GDN_D]
                    ss = jnp.sum(yg * yg, axis=-1, keepdims=True)
                    groups.append(yg * (lax.rsqrt(ss + L2_EPS) * post))
                y = jnp.concatenate(groups, axis=1)
        elif c0 < COL_Q_B or COL_Z_B <= c0 < COL_GATE_A:
            y = _silu(acc)
        elif c0 < COL_Z_B:
            y = acc * (LOG2E * SWA_HD ** -0.5)
        elif c0 < COL_KV_B:
            y = _sigmoid(acc + bg_ref[:, c0 - COL_GATE_A:c0 - COL_GATE_A + PROJ_TN])
        else:
            y = acc
        p_ref[:, cols] = y.astype(BF16)

    acc = matmul(order[0])
    for i, j in enumerate(order):
        nxt = matmul(order[i + 1]) if i + 1 < len(order) else None
        epilogue(j, acc)
        acc = nxt


def _project(x2d, w_main, w_bd, conv_w, b_gate, halo0, tm, tiles_per_seq):
    m = x2d.shape[0]
    n_tiles = m // tm
    const = lambda i: (0, 0)
    resident = dict(pipeline_mode=pl.Buffered(1))
    return pl.pallas_call(
        functools.partial(_proj_kernel, tm=tm, tiles_per_seq=tiles_per_seq),
        grid=(n_tiles,),
        in_specs=[
            pl.BlockSpec((tm, D_MODEL), lambda i: (i, 0)),
            pl.BlockSpec((D_MODEL, PROJ_W), const, **resident),
            pl.BlockSpec((D_MODEL, BD_W), const, **resident),
            pl.BlockSpec((GDN_CONV, CONV_W), const, **resident),
            pl.BlockSpec((1, 2 * D_MODEL), const, **resident),
            pl.BlockSpec((CONV_HALO, CONV_W), const, **resident),
        ],
        out_specs=[
            pl.BlockSpec((tm, PROJ_W), lambda i: (i, 0)),
            pl.BlockSpec((tm, BD_W), lambda i: (i, 0)),
            pl.BlockSpec((CONV_HALO, CONV_W), lambda i: (i, 0)),
        ],
        out_shape=[
            jax.ShapeDtypeStruct((m, PROJ_W), BF16),
            jax.ShapeDtypeStruct((m, BD_W), F32),
            jax.ShapeDtypeStruct((n_tiles * CONV_HALO, CONV_W), F32),
        ],
        scratch_shapes=[pltpu.VMEM((CONV_HALO, CONV_W), F32), pltpu.VMEM((tm, D_MODEL), BF16)],
        compiler_params=pltpu.CompilerParams(
            dimension_semantics=("arbitrary",), vmem_limit_bytes=PROJ_VMEM_LIMIT),
        name="proj",
    )(x2d, w_main, w_bd, conv_w, b_gate, halo0)


GDN_SKEW = 2


GDN_PAIRS = GDN_HEADS // 2


def _gdn_consts(alog_ref, dtb_ref):
    ri = lax.broadcasted_iota(jnp.int32, (CHUNK, 2 * CHUNK), 0)
    lane = lax.broadcasted_iota(jnp.int32, (CHUNK, 2 * CHUNK), 1)
    ci = jnp.bitwise_and(lane, CHUNK - 1)
    levels = []
    s = 2
    while s < CHUNK:
        levels.append(((ri // (2 * s)) == (ci // (2 * s))) & ((ri // s) != (ci // s)))
        s *= 2
    r1 = lax.broadcasted_iota(jnp.int32, (CHUNK, CHUNK), 0)
    c1 = lax.broadcasted_iota(jnp.int32, (CHUNK, CHUNK), 1)
    return dict(tri=ri >= ci, strict=ri > ci, eye=(ri == ci).astype(F32),
                pair=(ri // 2) == (ci // 2), levels=levels, first=lane < CHUNK,
                lower_ones=(r1 >= c1).astype(F32), upper_ones=(r1 <= c1).astype(F32),
                neg_alpha=-jnp.exp(alog_ref[...]), dt_bias=dtb_ref[...])


def _block_diag(first, packed):
    zero = jnp.zeros_like(packed)
    return jnp.concatenate([jnp.where(first, packed, zero), jnp.where(first, zero, packed)], axis=0)


def _stack_diag(m0, m1):
    return jnp.concatenate([jnp.concatenate([m0, jnp.zeros_like(m1)], axis=1),
                            jnp.concatenate([jnp.zeros_like(m0), m1], axis=1)], axis=0)


def _gdn_chunk_stages(chunk, consts, result):
    qkv, bd, row_valid = chunk
    first = consts["first"]
    beta = _sigmoid(bd[:, 0:GDN_HEADS])
    g = consts["neg_alpha"] * _softplus(bd[:, GDN_HEADS:2 * GDN_HEADS] + consts["dt_bias"])
    if row_valid is not None:
        g = g * row_valid
    g_cum = _mm_f32(consts["lower_ones"], g)
    g_cum_t = _mm_tn_f32(g, consts["upper_ones"])
    yield

    heads = []
    for h in range(GDN_HEADS):
        q = qkv(h, 0).astype(F32)
        k_bf = qkv(h, 1)
        k = k_bf.astype(F32)
        v = qkv(h, 2).astype(F32)
        b_col = beta[:, h:h + 1]
        gc_col = g_cum[:, h:h + 1]
        gc_last = g_cum_t[h:h + 1, CHUNK - 1:CHUNK]
        e_col = jnp.exp(gc_col)
        k_beta = k * b_col
        heads.append(dict(
            k=k_bf, k_beta=k_beta.astype(BF16), q=qkv(h, 0),
            rhs=jnp.concatenate([v * b_col, k_beta * e_col], axis=1).astype(BF16),
            q_dec=(q * e_col).astype(BF16),
            k_dec=(k * jnp.exp(gc_last - gc_col)).astype(BF16),
            g_last=jnp.exp(gc_last)))
    pairs = []
    for p in range(GDN_PAIRS):
        h0, h1 = heads[2 * p], heads[2 * p + 1]
        gc_col = jnp.where(first, g_cum[:, 2 * p:2 * p + 1], g_cum[:, 2 * p + 1:2 * p + 2])
        gc_row = jnp.concatenate([g_cum_t[2 * p:2 * p + 1, :], g_cum_t[2 * p + 1:2 * p + 2, :]], axis=1)
        lhs = jnp.concatenate([jnp.concatenate([h0["k_beta"], h1["k_beta"]], axis=1),
                               jnp.concatenate([h0["q"], h1["q"]], axis=1)], axis=0)
        pairs.append(dict(
            decay=jnp.where(consts["tri"], jnp.exp(gc_col - gc_row), 0.0),
            kq=_mm_nt(lhs, _stack_diag(h0["k"], h1["k"]))))
    yield

    for pr in pairs:
        a = jnp.where(consts["strict"], pr["kq"][:CHUNK] * pr["decay"], 0.0)
        pr["a"] = _block_diag(first, a.astype(BF16))
        pr["attn"] = (pr["kq"][CHUNK:] * pr["decay"]).astype(BF16)
        pr["t"] = consts["eye"] - jnp.where(consts["pair"], a, 0.0)
    for off in consts["levels"]:
        for pr in pairs:
            pr["x"] = _mm(pr["t"], pr["a"])
        yield
        for pr in pairs:
            pr["x"] = _mm(pr["x"], _block_diag(first, pr["t"].astype(BF16)))
        yield
        for pr in pairs:
            pr["t"] = pr["t"] - jnp.where(off, pr["x"], 0.0)

    out = dict(u=[], w_q=[], k_dec=[], g_last=[], attn=[pr["attn"] for pr in pairs])
    for p, pr in enumerate(pairs):
        h0, h1 = heads[2 * p], heads[2 * p + 1]
        uw = _mm(pr["t"], _stack_diag(h0["rhs"], h1["rhs"]))
        for i, hd in enumerate((h0, h1)):
            base = 2 * i * GDN_D
            out["u"].append(uw[:, base:base + GDN_D])
            out["w_q"].append(jnp.concatenate([uw[:, base + GDN_D:base + 2 * GDN_D].astype(BF16), hd["q_dec"]],
                                              axis=0))
            out["k_dec"].append(hd["k_dec"])
            out["g_last"].append(hd["g_last"])
    result.update(out)


def _gdn_scan_stages(prep, state, result):
    r = [_mm(w_q, s) for w_q, s in zip(prep["w_q"], state)]
    yield
    v_new = [(u - r_h[:CHUNK]).astype(BF16) for u, r_h in zip(prep["u"], r)]
    av = []
    for p, attn in enumerate(prep["attn"]):
        both = _mm(attn, _stack_diag(v_new[2 * p], v_new[2 * p + 1]))
        av += [both[:, :GDN_D], both[:, GDN_D:]]
    ds = [_mm_tn(k_dec, vn) for k_dec, vn in zip(prep["k_dec"], v_new)]
    result["outs"] = [r_h[CHUNK:] + av_h for r_h, av_h in zip(r, av)]
    result["state"] = [s * g_last + ds_h for g_last, s, ds_h in zip(prep["g_last"], state, ds)]


def _gdn_run(chunks, state, alog_ref, dtb_ref, on_output):
    consts = _gdn_consts(alog_ref, dtb_ref)
    n = len(chunks)
    prepared = [{} for _ in range(n)]
    local = [_gdn_chunk_stages(chunks[c], consts, prepared[c]) for c in range(n)]
    running = [True] * n
    scan, scan_result, scan_chunk = None, None, 0
    slot = 0
    while scan_chunk < n:
        for c in range(n):
            if running[c] and slot >= GDN_SKEW * c:
                running[c] = next(local[c], "done") != "done"
        if scan is None and not running[scan_chunk]:
            scan_result = {}
            scan = _gdn_scan_stages(prepared[scan_chunk], state, scan_result)
        if scan is not None and next(scan, "done") == "done":
            state = scan_result["state"]
            on_output(scan_chunk, scan_result["outs"])
            scan, scan_chunk = None, scan_chunk + 1
        slot += 1
    return state


def _head_cols(part, h):
    col = part * GDN_W + h * GDN_D
    return slice(col, col + GDN_D)


def _gdn_meta_kernel(pm_ref, bdm_ref, alog_ref, dtb_ref, s0_ref):
    rows = lax.broadcasted_iota(jnp.int32, (CHUNK, 1), 0)
    row_valid = (rows >= META_PAD).astype(F32)
    qkv = lambda h, part: pm_ref[:, _head_cols(part, h)]
    state = _gdn_run([(qkv, bdm_ref[...], row_valid)], [jnp.zeros((GDN_D, GDN_D), F32)] * GDN_HEADS,
                     alog_ref, dtb_ref, lambda c, outs: None)
    for h in range(GDN_HEADS):
        s0_ref[h] = state[h]


def _gdn_kernel(pa_ref, bdx_ref, s0_ref, alog_ref, dtb_ref, nw_ref, y_ref, s_ref, *, chunks_per_tile):
    @pl.when(pl.program_id(1) == 0)
    def _():
        s_ref[...] = s0_ref[...]

    chunks = []
    for c in range(chunks_per_tile):
        rows = slice(c * CHUNK, (c + 1) * CHUNK)
        qkv = lambda h, part, rows=rows: pa_ref[0, rows, _head_cols(part, h)]
        chunks.append((qkv, bdx_ref[0, rows, :], None))

    def on_output(c, outs):
        rows = slice(c * CHUNK, (c + 1) * CHUNK)
        for h in range(GDN_HEADS):
            o = outs[h]
            o = o * lax.rsqrt(jnp.mean(o * o, axis=-1, keepdims=True) + RMS_EPS) * nw_ref[...]
            z = pa_ref[0, rows, _head_cols(3, h)].astype(F32)
            y_ref[0, rows, h * GDN_D:(h + 1) * GDN_D] = (o * z).astype(BF16)

    state = _gdn_run(chunks, [s_ref[h] for h in range(GDN_HEADS)], alog_ref, dtb_ref, on_output)
    for h in range(GDN_HEADS):
        s_ref[h] = state[h]


def _gdn(p_x, p_m, bd_x, bd_m, a_log, dt_bias, norm_w, chunks_per_tile=8):
    b, seq, _ = p_x.shape
    tile = chunks_per_tile * CHUNK
    const = lambda *_: (0, 0)
    params = pltpu.CompilerParams(dimension_semantics=("arbitrary", "arbitrary"),
                                  vmem_limit_bytes=VMEM_LIMIT)
    state_shape = (GDN_HEADS, GDN_D, GDN_D)

    s0 = pl.pallas_call(
        _gdn_meta_kernel,
        grid=(1, 1),
        in_specs=[
            pl.BlockSpec((CHUNK, CONV_W), const),
            pl.BlockSpec((CHUNK, BD_W), const),
            pl.BlockSpec((1, GDN_HEADS), const),
            pl.BlockSpec((1, GDN_HEADS), const),
        ],
        out_specs=pl.BlockSpec(state_shape, lambda *_: (0, 0, 0)),
        out_shape=jax.ShapeDtypeStruct(state_shape, F32),
        compiler_params=params,
        name="gdn_meta",
    )(p_m, bd_m, a_log, dt_bias)

    return pl.pallas_call(
        functools.partial(_gdn_kernel, chunks_per_tile=chunks_per_tile),
        grid=(b, seq // tile),
        in_specs=[
            pl.BlockSpec((1, tile, 4 * GDN_W), lambda i, t: (i, t, 0)),
            pl.BlockSpec((1, tile, BD_W), lambda i, t: (i, t, 0)),
            pl.BlockSpec(state_shape, lambda i, t: (0, 0, 0)),
            pl.BlockSpec((1, GDN_HEADS), const),
            pl.BlockSpec((1, GDN_HEADS), const),
            pl.BlockSpec((1, GDN_D), const),
        ],
        out_specs=pl.BlockSpec((1, tile, GDN_W), lambda i, t: (i, t, 0)),
        out_shape=jax.ShapeDtypeStruct((b, seq, GDN_W), BF16),
        scratch_shapes=[pltpu.VMEM(state_shape, F32)],
        compiler_params=params,
        name="gdn",
    )(p_x, bd_x, s0, a_log, dt_bias, norm_w)


SWA_BAND = (WINDOW_CHUNKS + 1) * CHUNK
SWA_KEYS = SWA_BAND + N_META
SWA_ROWS = SWA_GROUPS * CHUNK
LANES = 128
CHUNK_SHIFT = CHUNK.bit_length() - 1
SWA_AHEAD = 4


def _swa_tables(bias_ref, mterm_ref):
    qi = lax.broadcasted_iota(jnp.int32, (SWA_ROWS, SWA_KEYS), 0)
    kj = lax.broadcasted_iota(jnp.int32, (SWA_ROWS, SWA_KEYS), 1)
    q_in_chunk = jnp.bitwise_and(qi, CHUNK - 1)
    group = lax.shift_right_logical(qi, CHUNK_SHIFT)
    key_chunk = lax.shift_right_logical(kj, CHUNK_SHIFT)
    key_row = jnp.bitwise_and(kj, CHUNK - 1)
    is_meta = kj >= SWA_BAND
    for kvh in range(SWA_KV_HEADS):
        slope = jnp.zeros((SWA_ROWS, SWA_KEYS), F32)
        for g in range(SWA_GROUPS):
            head = kvh * SWA_GROUPS + g
            slope = jnp.where(group == g, LOG2E * 2.0 ** (-8.0 * (head + 1) / SWA_HEADS), slope)
        meta_bias = -slope * (q_in_chunk + N_META - (kj - SWA_BAND)).astype(F32)
        for e in range(WINDOW_CHUNKS + 1):
            d = (e - key_chunk) * CHUNK + q_in_chunk - key_row
            band_bias = jnp.where(key_chunk <= e, -slope * jnp.abs(d).astype(F32), -jnp.inf)
            bias_ref[e, kvh] = jnp.where(is_meta, meta_bias, band_bias)
        mterm_ref[kvh] = jnp.where(is_meta, slope * CHUNK, 0.0)[:, LANES:]


def _swa_kernel(q_ref, z_ref, kv_ref, kvm_ref, sinks_ref, y_ref, bias_ref, mterm_ref, *, chunks_per_tile):
    t = pl.program_id(1)

    @pl.when((pl.program_id(0) == 0) & (t == 0))
    def _():
        _swa_tables(bias_ref, mterm_ref)

    group = lax.shift_right_logical(lax.broadcasted_iota(jnp.int32, (SWA_ROWS, 1), 0), CHUNK_SHIFT)
    sink_cols = []
    for kvh in range(SWA_KV_HEADS):
        sink = jnp.zeros((SWA_ROWS, 1), F32)
        for g in range(SWA_GROUPS):
            sink = jnp.where(group == g, sinks_ref[0, kvh * SWA_GROUPS + g] * LOG2E, sink)
        sink_cols.append(sink)
    kv_meta = kvm_ref[META_PAD:CHUNK, :]

    def scores(c, kvh):
        m = t * chunks_per_tile + c
        k0 = pl.multiple_of(jnp.maximum(m - WINDOW_CHUNKS, 0) * CHUNK, CHUNK)
        kv_win = kv_ref[0, pl.ds(k0, SWA_BAND), :]
        kcol = kvh * SWA_HD
        vcol = SWA_KV_W + kvh * SWA_HD
        k_win = jnp.concatenate([kv_win[:, kcol:kcol + SWA_HD], kv_meta[:, kcol:kcol + SWA_HD]], axis=0)
        v_win = jnp.concatenate([kv_win[:, vcol:vcol + SWA_HD], kv_meta[:, vcol:vcol + SWA_HD]], axis=0)
        rows = slice(c * CHUNK, (c + 1) * CHUNK)
        q_stack = jnp.concatenate(
            [q_ref[0, rows, (kvh * SWA_GROUPS + g) * SWA_HD:(kvh * SWA_GROUPS + g + 1) * SWA_HD]
             for g in range(SWA_GROUPS)], axis=0)
        return _mm_nt(q_stack, k_win), v_win, m

    def softmax(kvh, s, m):
        s = s + bias_ref[jnp.minimum(m, WINDOW_CHUNKS), kvh]
        s = jnp.concatenate([s[:, :LANES], s[:, LANES:] - mterm_ref[kvh] * m.astype(F32)], axis=1)
        sink = sink_cols[kvh]
        mx = jnp.maximum(jnp.max(s, axis=-1, keepdims=True), sink)
        return jnp.exp2(s - mx).astype(BF16), jnp.exp2(sink - mx)

    def store(c, kvh, o, key_sum, sink_term):
        o = o * (1.0 / (key_sum[:, :SWA_HD] + sink_term))
        o = jnp.concatenate([o[g * CHUNK:(g + 1) * CHUNK, :] for g in range(SWA_GROUPS)], axis=1)
        rows = slice(c * CHUNK, (c + 1) * CHUNK)
        cols = slice(kvh * SWA_GROUPS * SWA_HD, (kvh + 1) * SWA_GROUPS * SWA_HD)
        y_ref[0, rows, cols] = (o * z_ref[0, rows, cols].astype(F32)).astype(BF16)

    order = [(c, kvh) for c in range(chunks_per_tile) for kvh in range(SWA_KV_HEADS)]
    scored = [scores(c, kvh) for c, kvh in order]
    probs = [softmax(kvh, s, m) for (c, kvh), (s, _, m) in zip(order, scored)]
    outs = [_mm(e, v_win) for (e, _), (_, v_win, _) in zip(probs, scored)]
    ones = jnp.ones((SWA_KEYS, LANES), BF16)
    sums = [_mm(e, ones) for e, _ in probs]
    for (c, kvh), o, key_sum, (_, sink_term) in zip(order, outs, sums, probs):
        store(c, kvh, o, key_sum, sink_term)


def _swa(p_x, p_m, sinks, chunks_per_tile=4):
    b, seq, _ = p_x.shape
    tile = chunks_per_tile * CHUNK
    return pl.pallas_call(
        functools.partial(_swa_kernel, chunks_per_tile=chunks_per_tile),
        grid=(b, seq // tile),
        in_specs=[
            pl.BlockSpec((1, tile, SWA_W), lambda i, t: (i, t, COL_Q_B // SWA_W)),
            pl.BlockSpec((1, tile, SWA_W), lambda i, t: (i, t, COL_Z_B // SWA_W)),
            pl.BlockSpec((1, seq, 2 * SWA_KV_W), lambda i, t: (i, 0, COL_KV_B // (2 * SWA_KV_W))),
            pl.BlockSpec((CHUNK, 2 * SWA_KV_W), lambda i, t: (0, COL_KV_B // (2 * SWA_KV_W))),
            pl.BlockSpec(memory_space=pltpu.SMEM),
        ],
        out_specs=pl.BlockSpec((1, tile, SWA_W), lambda i, t: (i, t, 0)),
        out_shape=jax.ShapeDtypeStruct((b, seq, SWA_W), BF16),
        scratch_shapes=[
            pltpu.VMEM((WINDOW_CHUNKS + 1, SWA_KV_HEADS, SWA_ROWS, SWA_KEYS), F32),
            pltpu.VMEM((SWA_KV_HEADS, SWA_ROWS, SWA_KEYS - LANES), F32),
        ],
        compiler_params=pltpu.CompilerParams(
            dimension_semantics=("arbitrary", "arbitrary"), vmem_limit_bytes=VMEM_LIMIT),
        name="swa",
    )(p_x, p_x, p_x, p_m, sinks)


def _out_kernel(x_ref, ya_ref, yb_ref, ga_ref, gb_ref, wa_ref, wb_ref, wo_ref, lnw_ref, lnb_ref, o_ref):
    mixed = (ga_ref[...].astype(F32) * _mm(ya_ref[...], wa_ref[...])
             + gb_ref[...].astype(F32) * _mm(yb_ref[...], wb_ref[...]))
    r = DEEPNORM_ALPHA * x_ref[...] + _mm(mixed, wo_ref[...])
    mu = jnp.mean(r, axis=-1, keepdims=True)
    d = r - mu
    var = jnp.mean(d * d, axis=-1, keepdims=True)
    o_ref[...] = d * lax.rsqrt(var + LN_EPS) * lnw_ref[...] + lnb_ref[...]


def _output(x2d, y_a, y_b, p_x2d, w_a, w_b, w_o, ln_w, ln_b, tm=512):
    m = x2d.shape[0]
    row = lambda i: (i, 0)
    const = lambda i: (0, 0)
    return pl.pallas_call(
        _out_kernel,
        grid=(m // tm,),
        in_specs=[
            pl.BlockSpec((tm, D_MODEL), row),
            pl.BlockSpec((tm, D_MODEL), row),
            pl.BlockSpec((tm, D_MODEL), row),
            pl.BlockSpec((tm, D_MODEL), lambda i: (i, COL_GATE_A // D_MODEL)),
            pl.BlockSpec((tm, D_MODEL), lambda i: (i, COL_GATE_B // D_MODEL)),
            pl.BlockSpec((D_MODEL, D_MODEL), const),
            pl.BlockSpec((D_MODEL, D_MODEL), const),
            pl.BlockSpec((D_MODEL, D_MODEL), const),
            pl.BlockSpec((1, D_MODEL), const),
            pl.BlockSpec((1, D_MODEL), const),
        ],
        out_specs=pl.BlockSpec((tm, D_MODEL), row),
        out_shape=jax.ShapeDtypeStruct((m, D_MODEL), F32),
        compiler_params=pltpu.CompilerParams(
            dimension_semantics=("arbitrary",), vmem_limit_bytes=VMEM_LIMIT),
        name="merge_out",
    )(x2d, y_a, y_b, p_x2d, p_x2d, w_a, w_b, w_o, ln_w, ln_b)


def _one_layer(x, meta_tokens, w_in, b_gate, conv_w, a_log, dt_bias, gdn_norm_w, sinks,
               w_proj_a, w_proj_b, w_out, ln_w, ln_b):
    b, seq, _ = x.shape
    o_bd = 4 * GDN_W
    o_qb = o_bd + 2 * GDN_HEADS
    o_kb = o_qb + SWA_W
    o_zb = o_kb + 2 * SWA_KV_W
    w_main = jnp.concatenate(
        [w_in[:, :o_bd], w_in[:, o_qb:o_kb], w_in[:, o_zb:], w_in[:, o_kb:o_zb]], axis=1).astype(BF16)
    w_bd = jnp.pad(w_in[:, o_bd:o_qb], ((0, 0), (0, BD_W - 2 * GDN_HEADS))).astype(BF16)
    b_gate = b_gate.reshape(1, 2 * D_MODEL)

    x2d = x.reshape(b * seq, D_MODEL)
    meta_chunk = jnp.concatenate([jnp.zeros((META_PAD, D_MODEL), x.dtype), meta_tokens.astype(x.dtype)], axis=0)

    no_halo = jnp.zeros((CONV_HALO, CONV_W), F32)
    p_m, bd_m, meta_tail = _project(meta_chunk, w_main, w_bd, conv_w, b_gate, no_halo,
                                    tm=CHUNK, tiles_per_seq=1)
    p_x2d, bd_x2d, _ = _project(x2d, w_main, w_bd, conv_w, b_gate, meta_tail,
                                tm=PROJ_TM, tiles_per_seq=seq // PROJ_TM)
    p_x = p_x2d.reshape(b, seq, PROJ_W)
    bd_x = bd_x2d.reshape(b, seq, BD_W)

    y_a = _gdn(p_x, p_m, bd_x, bd_m, a_log.reshape(1, GDN_HEADS), dt_bias.reshape(1, GDN_HEADS),
               gdn_norm_w.reshape(1, GDN_D))
    y_b = _swa(p_x, p_m, sinks.reshape(1, SWA_HEADS))

    out = _output(x2d, y_a.reshape(b * seq, GDN_W), y_b.reshape(b * seq, SWA_W), p_x2d,
                  w_proj_a.astype(BF16), w_proj_b.astype(BF16), w_out.astype(BF16),
                  ln_w.reshape(1, D_MODEL), ln_b.reshape(1, D_MODEL))
    return out.reshape(b, seq, D_MODEL)


def kernel(x, meta_tokens, w_in, b_gate, conv_w, a_log, dt_bias, gdn_norm_w, sinks,
           w_proj_a, w_proj_b, w_out, ln_w, ln_b):
    depth = w_in.shape[0]
    assert depth == 1, "meta tokens are projected once; deeper stacks need per-layer meta rows"
    return _one_layer(x, meta_tokens, w_in[0], b_gate[0], conv_w[0], a_log[0], dt_bias[0],
                      gdn_norm_w[0], sinks[0], w_proj_a[0], w_proj_b[0], w_out[0], ln_w[0], ln_b[0])
```

```python
import functools

import jax
import jax.numpy as jnp
from jax import lax
from jax.experimental import pallas as pl
from jax.experimental.pallas import tpu as pltpu

F32 = jnp.float32
BF16 = jnp.bfloat16

D_MODEL = 1024
CHUNK = 64
N_META = 16
META_PAD = CHUNK - N_META
GDN_HEADS = 8
GDN_D = 128
GDN_CONV = 4
GDN_W = GDN_HEADS * GDN_D
SWA_HEADS = 16
SWA_KV_HEADS = 4
SWA_GROUPS = SWA_HEADS // SWA_KV_HEADS
SWA_HD = 64
SWA_W = SWA_HEADS * SWA_HD
SWA_KV_W = SWA_KV_HEADS * SWA_HD
WINDOW_CHUNKS = 2
DEEPNORM_ALPHA = 2.0 ** 0.25
LN_EPS = 1e-5
RMS_EPS = 1e-6
L2_EPS = 1e-6
LOG2E = 1.4426950408889634

COL_Q_A = 0
COL_K_A = GDN_W
COL_V_A = 2 * GDN_W
COL_Z_A = 3 * GDN_W
COL_Q_B = 4 * GDN_W
COL_KV_B = COL_Q_B + SWA_W
COL_Z_B = COL_KV_B + 2 * SWA_KV_W
COL_GATE_A = COL_Z_B + SWA_W
COL_GATE_B = COL_GATE_A + D_MODEL
PROJ_W = COL_GATE_B + D_MODEL
HALF = 512
CONV_W = 3 * GDN_W
BD_W = 128
W_SEGMENTS = ((COL_Q_A, COL_Q_B), (COL_Q_B, PROJ_W - COL_Q_B))

PROJ_TN = 256
PROJ_TM = 512
CONV_HALO = 8
VMEM_LIMIT = 48 * 1024 * 1024
PROJ_VMEM_LIMIT = 56 * 1024 * 1024


def _mm(a, b):
    return jnp.dot(a.astype(BF16), b.astype(BF16), preferred_element_type=F32)


def _mm_nt(a, b):
    return lax.dot_general(a.astype(BF16), b.astype(BF16), (((1,), (1,)), ((), ())),
                           preferred_element_type=F32)


def _mm_tn(a, b):
    return lax.dot_general(a.astype(BF16), b.astype(BF16), (((0,), (0,)), ((), ())),
                           preferred_element_type=F32)


def _mm_f32(a, b):
    return jnp.dot(a, b, preferred_element_type=F32, precision=lax.Precision.HIGHEST)


def _mm_tn_f32(a, b):
    return lax.dot_general(a, b, (((0,), (0,)), ((), ())), preferred_element_type=F32,
                           precision=lax.Precision.HIGHEST)


def _sigmoid(x):
    return 1.0 / (1.0 + jnp.exp2(x * -LOG2E))


SUBLANES = 8


def _shift_rows(a, k):
    rows, width = a.shape
    tiles = a.reshape(rows // SUBLANES, SUBLANES, width)
    rolled = pltpu.roll(tiles, k, axis=1)
    above = jnp.concatenate([rolled[:1], rolled[:-1]], axis=0)
    sub = lax.broadcasted_iota(jnp.int32, tiles.shape, 1)
    return jnp.where(sub < k, above, rolled).reshape(rows, width)


def _silu(x):
    return x * _sigmoid(x)


def _softplus(x):
    return jnp.maximum(x, 0.0) + jnp.log(1.0 + jnp.exp(-jnp.abs(x)))


def _proj_kernel(x_ref, w0_ref, w1_ref, wbd_ref, convw_ref, bg_ref, halo0_ref,
                 p_ref, bd_ref, tail_ref, carry_ref, xb_ref, *, tm, tiles_per_seq):
    first_of_seq = (pl.program_id(0) % tiles_per_seq) == 0
    xb_ref[...] = x_ref[...].astype(BF16)
    bd_ref[...] = jnp.dot(xb_ref[...], wbd_ref[...], preferred_element_type=F32)

    n_tiles = PROJ_W // PROJ_TN
    heavy = list(range(CONV_W // PROJ_TN))
    light = list(range(CONV_W // PROJ_TN, n_tiles))
    order = []
    while heavy or light:
        order += heavy[:1] + light[:2]
        heavy, light = heavy[1:], light[2:]

    def matmul(j):
        c0 = j * PROJ_TN
        for (start, width), w_ref in zip(W_SEGMENTS, (w0_ref, w1_ref)):
            if start <= c0 < start + width:
                return jnp.dot(xb_ref[...], w_ref[:, c0 - start:c0 - start + PROJ_TN],
                               preferred_element_type=F32)

    def epilogue(j, acc):
        c0 = j * PROJ_TN
        cols = slice(c0, c0 + PROJ_TN)
        if c0 < CONV_W:
            halo = jnp.where(first_of_seq, halo0_ref[:, cols], carry_ref[:, cols])
            tail = acc[tm - CONV_HALO:tm, :]
            carry_ref[:, cols] = tail
            tail_ref[:, cols] = tail
            raw = jnp.concatenate([halo, acc], axis=0)
            w0, w1, w2, w3 = (convw_ref[tap:tap + 1, cols] for tap in range(GDN_CONV))
            back1 = _shift_rows(raw, 1)
            u = raw * w1 + back1 * w0
            y = (acc * w3 + back1[CONV_HALO:, :] * w2) + _shift_rows(u, 2)[CONV_HALO:, :]
            y = _silu(y)
            if c0 < COL_V_A:
                post = GDN_D ** -0.5 if c0 < COL_K_A else 1.0
                groups = []
                for g in range(PROJ_TN // GDN_D):
                    yg = y[:, g * GDN_D:(g + 1) * GDN_D]
                    ss = jnp.sum(yg * yg, axis=-1, keepdims=True)
                    groups.append(yg * (lax.rsqrt(ss + L2_EPS) * post))
                y = jnp.concatenate(groups, axis=1)
        elif c0 < COL_Q_B or COL_Z_B <= c0 < COL_GATE_A:
            y = _silu(acc)
        elif c0 < COL_KV_B:
            y = acc * (LOG2E * SWA_HD ** -0.5)
        elif c0 < COL_Z_B:
            y = acc
        else:
            y = _sigmoid(acc + bg_ref[:, c0 - COL_GATE_A:c0 - COL_GATE_A + PROJ_TN])
        p_ref[:, cols] = y.astype(BF16)

    acc = matmul(order[0])
    for i, j in enumerate(order):
        nxt = matmul(order[i + 1]) if i + 1 < len(order) else None
        epilogue(j, acc)
        acc = nxt


def _project(x2d, w_segments, w_bd, conv_w, b_gate, halo0, tm, tiles_per_seq):
    m = x2d.shape[0]
    n_tiles = m // tm
    const = lambda i: (0, 0)
    resident = dict(pipeline_mode=pl.Buffered(1))
    return pl.pallas_call(
        functools.partial(_proj_kernel, tm=tm, tiles_per_seq=tiles_per_seq),
        grid=(n_tiles,),
        in_specs=[
            pl.BlockSpec((tm, D_MODEL), lambda i: (i, 0)),
            *[pl.BlockSpec((D_MODEL, width), const, **resident) for _, width in W_SEGMENTS],
            pl.BlockSpec((D_MODEL, BD_W), const, **resident),
            pl.BlockSpec((GDN_CONV, CONV_W), const, **resident),
            pl.BlockSpec((1, 2 * D_MODEL), const, **resident),
            pl.BlockSpec((CONV_HALO, CONV_W), const, **resident),
        ],
        out_specs=[
            pl.BlockSpec((tm, PROJ_W), lambda i: (i, 0)),
            pl.BlockSpec((tm, BD_W), lambda i: (i, 0)),
            pl.BlockSpec((CONV_HALO, CONV_W), lambda i: (i, 0)),
        ],
        out_shape=[
            jax.ShapeDtypeStruct((m, PROJ_W), BF16),
            jax.ShapeDtypeStruct((m, BD_W), F32),
            jax.ShapeDtypeStruct((n_tiles * CONV_HALO, CONV_W), F32),
        ],
        scratch_shapes=[pltpu.VMEM((CONV_HALO, CONV_W), F32), pltpu.VMEM((tm, D_MODEL), BF16)],
        compiler_params=pltpu.CompilerParams(
            dimension_semantics=("arbitrary",), vmem_limit_bytes=PROJ_VMEM_LIMIT),
        name="proj",
    )(x2d, *w_segments, w_bd, conv_w, b_gate, halo0)


GDN_SKEW = 2


GDN_PAIRS = GDN_HEADS // 2


def _gdn_consts(alog_ref, dtb_ref):
    ri = lax.broadcasted_iota(jnp.int32, (CHUNK, 2 * CHUNK), 0)
    lane = lax.broadcasted_iota(jnp.int32, (CHUNK, 2 * CHUNK), 1)
    ci = jnp.bitwise_and(lane, CHUNK - 1)
    levels = []
    s = 2
    while s < CHUNK:
        levels.append(((ri // (2 * s)) == (ci // (2 * s))) & ((ri // s) != (ci // s)))
        s *= 2
    r1 = lax.broadcasted_iota(jnp.int32, (CHUNK, CHUNK), 0)
    c1 = lax.broadcasted_iota(jnp.int32, (CHUNK, CHUNK), 1)
    return dict(tri=ri >= ci, strict=ri > ci, eye=(ri == ci).astype(F32),
                pair=(ri // 2) == (ci // 2), levels=levels, first=lane < CHUNK,
                lower_ones=(r1 >= c1).astype(F32), upper_ones=(r1 <= c1).astype(F32),
                neg_alpha=-jnp.exp(alog_ref[...]), dt_bias=dtb_ref[...])


def _block_diag(first, packed):
    zero = jnp.zeros_like(packed)
    return jnp.concatenate([jnp.where(first, packed, zero), jnp.where(first, zero, packed)], axis=0)


def _stack_diag(m0, m1):
    return jnp.concatenate([jnp.concatenate([m0, jnp.zeros_like(m1)], axis=1),
                            jnp.concatenate([jnp.zeros_like(m0), m1], axis=1)], axis=0)


def _gdn_chunk_stages(chunk, consts, result):
    qkv, bd, row_valid = chunk
    first = consts["first"]
    beta = _sigmoid(bd[:, 0:GDN_HEADS])
    g = consts["neg_alpha"] * _softplus(bd[:, GDN_HEADS:2 * GDN_HEADS] + consts["dt_bias"])
    if row_valid is not None:
        g = g * row_valid
    g_cum = _mm_f32(consts["lower_ones"], g)
    g_cum_t = _mm_tn_f32(g, consts["upper_ones"])
    yield

    heads = []
    for h in range(GDN_HEADS):
        q = qkv(h, 0).astype(F32)
        k_bf = qkv(h, 1)
        k = k_bf.astype(F32)
        v = qkv(h, 2).astype(F32)
        b_col = beta[:, h:h + 1]
        gc_col = g_cum[:, h:h + 1]
        gc_last = g_cum_t[h:h + 1, CHUNK - 1:CHUNK]
        e_col = jnp.exp(gc_col)
        k_beta = k * b_col
        heads.append(dict(
            k=k_bf, k_beta=k_beta.astype(BF16), q=qkv(h, 0),
            rhs=jnp.concatenate([v * b_col, k_beta * e_col], axis=1).astype(BF16),
            q_dec=(q * e_col).astype(BF16),
            k_dec=(k * jnp.exp(gc_last - gc_col)).astype(BF16),
            g_last=jnp.exp(gc_last)))
    pairs = []
    for p in range(GDN_PAIRS):
        h0, h1 = heads[2 * p], heads[2 * p + 1]
        gc_col = jnp.where(first, g_cum[:, 2 * p:2 * p + 1], g_cum[:, 2 * p + 1:2 * p + 2])
        gc_row = jnp.concatenate([g_cum_t[2 * p:2 * p + 1, :], g_cum_t[2 * p + 1:2 * p + 2, :]], axis=1)
        lhs = jnp.concatenate([jnp.concatenate([h0["k_beta"], h1["k_beta"]], axis=1),
                               jnp.concatenate([h0["q"], h1["q"]], axis=1)], axis=0)
        pairs.append(dict(
            decay=jnp.where(consts["tri"], jnp.exp(gc_col - gc_row), 0.0),
            kq=_mm_nt(lhs, _stack_diag(h0["k"], h1["k"]))))
    yield

    for pr in pairs:
        a = jnp.where(consts["strict"], pr["kq"][:CHUNK] * pr["decay"], 0.0)
        pr["a"] = _block_diag(first, a.astype(BF16))
        pr["attn"] = (pr["kq"][CHUNK:] * pr["decay"]).astype(BF16)
        pr["t"] = consts["eye"] - jnp.where(consts["pair"], a, 0.0)
    for off in consts["levels"]:
        for pr in pairs:
            pr["x"] = _mm(pr["t"], pr["a"])
        yield
        for pr in pairs:
            pr["x"] = _mm(pr["x"], _block_diag(first, pr["t"].astype(BF16)))
        yield
        for pr in pairs:
            pr["t"] = pr["t"] - jnp.where(off, pr["x"], 0.0)

    out = dict(u=[], w_q=[], k_dec=[], g_last=[], attn=[pr["attn"] for pr in pairs])
    for p, pr in enumerate(pairs):
        h0, h1 = heads[2 * p], heads[2 * p + 1]
        uw = _mm(pr["t"], _stack_diag(h0["rhs"], h1["rhs"]))
        for i, hd in enumerate((h0, h1)):
            base = 2 * i * GDN_D
            out["u"].append(uw[:, base:base + GDN_D])
            out["w_q"].append(jnp.concatenate([uw[:, base + GDN_D:base + 2 * GDN_D].astype(BF16), hd["q_dec"]],
                                              axis=0))
            out["k_dec"].append(hd["k_dec"])
            out["g_last"].append(hd["g_last"])
    result.update(out)


def _gdn_scan_stages(prep, state, result):
    r = [_mm(w_q, s) for w_q, s in zip(prep["w_q"], state)]
    yield
    v_new = [(u - r_h[:CHUNK]).astype(BF16) for u, r_h in zip(prep["u"], r)]
    av = []
    for p, attn in enumerate(prep["attn"]):
        both = _mm(attn, _stack_diag(v_new[2 * p], v_new[2 * p + 1]))
        av += [both[:, :GDN_D], both[:, GDN_D:]]
    ds = [_mm_tn(k_dec, vn) for k_dec, vn in zip(prep["k_dec"], v_new)]
    result["outs"] = [r_h[CHUNK:] + av_h for r_h, av_h in zip(r, av)]
    result["state"] = [s * g_last + ds_h for g_last, s, ds_h in zip(prep["g_last"], state, ds)]


def _gdn_run(chunks, state, alog_ref, dtb_ref, on_output):
    consts = _gdn_consts(alog_ref, dtb_ref)
    n = len(chunks)
    prepared = [{} for _ in range(n)]
    local = [_gdn_chunk_stages(chunks[c], consts, prepared[c]) for c in range(n)]
    running = [True] * n
    scan, scan_result, scan_chunk = None, None, 0
    slot = 0
    while scan_chunk < n:
        for c in range(n):
            if running[c] and slot >= GDN_SKEW * c:
                running[c] = next(local[c], "done") != "done"
        if scan is None and not running[scan_chunk]:
            scan_result = {}
            scan = _gdn_scan_stages(prepared[scan_chunk], state, scan_result)
        if scan is not None and next(scan, "done") == "done":
            state = scan_result["state"]
            on_output(scan_chunk, scan_result["outs"])
            scan, scan_chunk = None, scan_chunk + 1
        slot += 1
    return state


def _head_cols(part, h):
    col = part * GDN_W + h * GDN_D
    return slice(col, col + GDN_D)


def _gdn_meta_kernel(pm_ref, bdm_ref, alog_ref, dtb_ref, s0_ref):
    rows = lax.broadcasted_iota(jnp.int32, (CHUNK, 1), 0)
    row_valid = (rows >= META_PAD).astype(F32)
    qkv = lambda h, part: pm_ref[:, _head_cols(part, h)]
    state = _gdn_run([(qkv, bdm_ref[...], row_valid)], [jnp.zeros((GDN_D, GDN_D), F32)] * GDN_HEADS,
                     alog_ref, dtb_ref, lambda c, outs: None)
    for h in range(GDN_HEADS):
        s0_ref[h] = state[h]


def _gdn_kernel(pa_ref, bdx_ref, s0_ref, alog_ref, dtb_ref, nw_ref, y_ref, s_ref, *, chunks_per_tile):
    @pl.when(pl.program_id(1) == 0)
    def _():
        s_ref[...] = s0_ref[...]

    chunks = []
    for c in range(chunks_per_tile):
        rows = slice(c * CHUNK, (c + 1) * CHUNK)
        qkv = lambda h, part, rows=rows: pa_ref[0, rows, _head_cols(part, h)]
        chunks.append((qkv, bdx_ref[0, rows, :], None))

    def on_output(c, outs):
        rows = slice(c * CHUNK, (c + 1) * CHUNK)
        for h in range(GDN_HEADS):
            o = outs[h]
            o = o * lax.rsqrt(jnp.mean(o * o, axis=-1, keepdims=True) + RMS_EPS) * nw_ref[...]
            z = pa_ref[0, rows, _head_cols(3, h)].astype(F32)
            y_ref[0, rows, h * GDN_D:(h + 1) * GDN_D] = (o * z).astype(BF16)

    state = _gdn_run(chunks, [s_ref[h] for h in range(GDN_HEADS)], alog_ref, dtb_ref, on_output)
    for h in range(GDN_HEADS):
        s_ref[h] = state[h]


def _gdn(p_x, p_m, bd_x, bd_m, a_log, dt_bias, norm_w, chunks_per_tile=8):
    b, seq, _ = p_x.shape
    tile = chunks_per_tile * CHUNK
    const = lambda *_: (0, 0)
    params = pltpu.CompilerParams(dimension_semantics=("arbitrary", "arbitrary"),
                                  vmem_limit_bytes=VMEM_LIMIT)
    state_shape = (GDN_HEADS, GDN_D, GDN_D)

    s0 = pl.pallas_call(
        _gdn_meta_kernel,
        grid=(1, 1),
        in_specs=[
            pl.BlockSpec((CHUNK, CONV_W), const),
            pl.BlockSpec((CHUNK, BD_W), const),
            pl.BlockSpec((1, GDN_HEADS), const),
            pl.BlockSpec((1, GDN_HEADS), const),
        ],
        out_specs=pl.BlockSpec(state_shape, lambda *_: (0, 0, 0)),
        out_shape=jax.ShapeDtypeStruct(state_shape, F32),
        compiler_params=params,
        name="gdn_meta",
    )(p_m, bd_m, a_log, dt_bias)

    return pl.pallas_call(
        functools.partial(_gdn_kernel, chunks_per_tile=chunks_per_tile),
        grid=(b, seq // tile),
        in_specs=[
            pl.BlockSpec((1, tile, 4 * GDN_W), lambda i, t: (i, t, 0)),
            pl.BlockSpec((1, tile, BD_W), lambda i, t: (i, t, 0)),
            pl.BlockSpec(state_shape, lambda i, t: (0, 0, 0)),
            pl.BlockSpec((1, GDN_HEADS), const),
            pl.BlockSpec((1, GDN_HEADS), const),
            pl.BlockSpec((1, GDN_D), const),
        ],
        out_specs=pl.BlockSpec((1, tile, GDN_W), lambda i, t: (i, t, 0)),
        out_shape=jax.ShapeDtypeStruct((b, seq, GDN_W), BF16),
        scratch_shapes=[pltpu.VMEM(state_shape, F32)],
        compiler_params=params,
        name="gdn",
    )(p_x, bd_x, s0, a_log, dt_bias, norm_w)


SWA_BAND = (WINDOW_CHUNKS + 1) * CHUNK
SWA_KEYS = SWA_BAND + N_META
SWA_ROWS = SWA_GROUPS * CHUNK
LANES = 128
CHUNK_SHIFT = CHUNK.bit_length() - 1
SWA_GROUP_SIZE = 4


def _swa_tables(bias_ref, mterm_ref):
    qi = lax.broadcasted_iota(jnp.int32, (SWA_ROWS, SWA_KEYS), 0)
    kj = lax.broadcasted_iota(jnp.int32, (SWA_ROWS, SWA_KEYS), 1)
    q_in_chunk = jnp.bitwise_and(qi, CHUNK - 1)
    group = lax.shift_right_logical(qi, CHUNK_SHIFT)
    key_chunk = lax.shift_right_logical(kj, CHUNK_SHIFT)
    key_row = jnp.bitwise_and(kj, CHUNK - 1)
    is_meta = kj >= SWA_BAND
    for kvh in range(SWA_KV_HEADS):
        slope = jnp.zeros((SWA_ROWS, SWA_KEYS), F32)
        for g in range(SWA_GROUPS):
            head = kvh * SWA_GROUPS + g
            slope = jnp.where(group == g, LOG2E * 2.0 ** (-8.0 * (head + 1) / SWA_HEADS), slope)
        meta_bias = -slope * (q_in_chunk + N_META - (kj - SWA_BAND)).astype(F32)
        for e in range(WINDOW_CHUNKS + 1):
            d = (e - key_chunk) * CHUNK + q_in_chunk - key_row
            band_bias = jnp.where(key_chunk <= e, -slope * jnp.abs(d).astype(F32), -jnp.inf)
            bias_ref[e, kvh] = jnp.where(is_meta, meta_bias, band_bias)
        mterm_ref[kvh] = jnp.where(is_meta, slope * CHUNK, 0.0)[:, LANES:]


def _swa_kernel(q_ref, z0_ref, z1_ref, kv_ref, kvm_ref, sinks_ref, y_ref, bias_ref, mterm_ref, *,
                chunks_per_tile):
    t = pl.program_id(1)

    @pl.when((pl.program_id(0) == 0) & (t == 0))
    def _():
        _swa_tables(bias_ref, mterm_ref)

    group = lax.shift_right_logical(lax.broadcasted_iota(jnp.int32, (SWA_ROWS, 1), 0), CHUNK_SHIFT)
    sink_cols = []
    for kvh in range(SWA_KV_HEADS):
        sink = jnp.zeros((SWA_ROWS, 1), F32)
        for g in range(SWA_GROUPS):
            sink = jnp.where(group == g, sinks_ref[0, kvh * SWA_GROUPS + g] * LOG2E, sink)
        sink_cols.append(sink)
    kv_meta = kvm_ref[META_PAD:CHUNK, :]

    def scores(c, kvh):
        m = t * chunks_per_tile + c
        k0 = pl.multiple_of(jnp.maximum(m - WINDOW_CHUNKS, 0) * CHUNK, CHUNK)
        kv_win = kv_ref[0, pl.ds(k0, SWA_BAND), :]
        kcol = kvh * SWA_HD
        vcol = SWA_KV_W + kvh * SWA_HD
        k_win = jnp.concatenate([kv_win[:, kcol:kcol + SWA_HD], kv_meta[:, kcol:kcol + SWA_HD]], axis=0)
        v_win = jnp.concatenate([kv_win[:, vcol:vcol + SWA_HD], kv_meta[:, vcol:vcol + SWA_HD]], axis=0)
        rows = slice(c * CHUNK, (c + 1) * CHUNK)
        q_stack = jnp.concatenate(
            [q_ref[0, rows, (kvh * SWA_GROUPS + g) * SWA_HD:(kvh * SWA_GROUPS + g + 1) * SWA_HD]
             for g in range(SWA_GROUPS)], axis=0)
        return _mm_nt(q_stack, k_win), v_win, m

    def softmax(kvh, s, m):
        s = s + bias_ref[jnp.minimum(m, WINDOW_CHUNKS), kvh]
        s = jnp.concatenate([s[:, :LANES], s[:, LANES:] - mterm_ref[kvh] * m.astype(F32)], axis=1)
        sink = sink_cols[kvh]
        mx = jnp.maximum(jnp.max(s, axis=-1, keepdims=True), sink)
        return jnp.exp2(s - mx).astype(BF16), jnp.exp2(sink - mx)

    def store(c, kvh, o, key_sum, sink_term):
        o = o * (1.0 / (key_sum[:, :SWA_HD] + sink_term))
        o = jnp.concatenate([o[g * CHUNK:(g + 1) * CHUNK, :] for g in range(SWA_GROUPS)], axis=1)
        rows = slice(c * CHUNK, (c + 1) * CHUNK)
        cols = slice(kvh * SWA_GROUPS * SWA_HD, (kvh + 1) * SWA_GROUPS * SWA_HD)
        z_ref = (z0_ref, z1_ref)[cols.start // HALF]
        z = z_ref[0, rows, cols.start % HALF:cols.start % HALF + SWA_GROUPS * SWA_HD]
        y_ref[0, rows, cols] = (o * z.astype(F32)).astype(BF16)

    ones = jnp.ones((SWA_KEYS, LANES), BF16)

    def finish(group, scored):
        probs = [softmax(kvh, s, m) for (c, kvh), (s, _, m) in zip(group, scored)]
        outs = [_mm(e, v_win) for (e, _), (_, v_win, _) in zip(probs, scored)]
        sums = [_mm(e, ones) for e, _ in probs]
        for (c, kvh), o, key_sum, (_, sink_term) in zip(group, outs, sums, probs):
            store(c, kvh, o, key_sum, sink_term)

    order = [(c, kvh) for c in range(chunks_per_tile) for kvh in range(SWA_KV_HEADS)]
    groups = [order[i:i + SWA_GROUP_SIZE] for i in range(0, len(order), SWA_GROUP_SIZE)]
    scored = [scores(c, kvh) for c, kvh in groups[0]]
    for g, group in enumerate(groups):
        current = scored
        if g + 1 < len(groups):
            scored = [scores(c, kvh) for c, kvh in groups[g + 1]]
        finish(group, current)


def _swa(p_x, p_m, sinks, chunks_per_tile=4):
    b, seq, _ = p_x.shape
    tile = chunks_per_tile * CHUNK
    return pl.pallas_call(
        functools.partial(_swa_kernel, chunks_per_tile=chunks_per_tile),
        grid=(b, seq // tile),
        in_specs=[
            pl.BlockSpec((1, tile, SWA_W), lambda i, t: (i, t, COL_Q_B // SWA_W)),
            pl.BlockSpec((1, tile, HALF), lambda i, t: (i, t, COL_Z_B // HALF)),
            pl.BlockSpec((1, tile, HALF), lambda i, t: (i, t, COL_Z_B // HALF + 1)),
            pl.BlockSpec((1, seq, 2 * SWA_KV_W), lambda i, t: (i, 0, COL_KV_B // (2 * SWA_KV_W))),
            pl.BlockSpec((CHUNK, 2 * SWA_KV_W), lambda i, t: (0, COL_KV_B // (2 * SWA_KV_W))),
            pl.BlockSpec(memory_space=pltpu.SMEM),
        ],
        out_specs=pl.BlockSpec((1, tile, SWA_W), lambda i, t: (i, t, 0)),
        out_shape=jax.ShapeDtypeStruct((b, seq, SWA_W), BF16),
        scratch_shapes=[
            pltpu.VMEM((WINDOW_CHUNKS + 1, SWA_KV_HEADS, SWA_ROWS, SWA_KEYS), F32),
            pltpu.VMEM((SWA_KV_HEADS, SWA_ROWS, SWA_KEYS - LANES), F32),
        ],
        compiler_params=pltpu.CompilerParams(
            dimension_semantics=("arbitrary", "arbitrary"), vmem_limit_bytes=VMEM_LIMIT),
        name="swa",
    )(p_x, p_x, p_x, p_x, p_m, sinks)


def _out_kernel(x_ref, ya_ref, yb_ref, ga0_ref, ga1_ref, gb0_ref, gb1_ref, wa_ref, wb_ref, wo_ref,
                lnw_ref, lnb_ref, o_ref, wa_bf, wb_bf, wo_bf):
    @pl.when(pl.program_id(0) == 0)
    def _():
        wa_bf[...] = wa_ref[...].astype(BF16)
        wb_bf[...] = wb_ref[...].astype(BF16)
        wo_bf[...] = wo_ref[...].astype(BF16)

    g_a = jnp.concatenate([ga0_ref[...], ga1_ref[...]], axis=1).astype(F32)
    g_b = jnp.concatenate([gb0_ref[...], gb1_ref[...]], axis=1).astype(F32)
    mixed = g_a * _mm(ya_ref[...], wa_bf[...]) + g_b * _mm(yb_ref[...], wb_bf[...])
    r = DEEPNORM_ALPHA * x_ref[...] + _mm(mixed, wo_bf[...])
    mu = jnp.mean(r, axis=-1, keepdims=True)
    d = r - mu
    var = jnp.mean(d * d, axis=-1, keepdims=True)
    o_ref[...] = d * lax.rsqrt(var + LN_EPS) * lnw_ref[...] + lnb_ref[...]


def _output(x2d, y_a, y_b, p_x2d, w_a, w_b, w_o, ln_w, ln_b, tm=512):
    m = x2d.shape[0]
    row = lambda i: (i, 0)
    const = lambda i: (0, 0)
    return pl.pallas_call(
        _out_kernel,
        grid=(m // tm,),
        in_specs=[
            pl.BlockSpec((tm, D_MODEL), row),
            pl.BlockSpec((tm, D_MODEL), row),
            pl.BlockSpec((tm, D_MODEL), row),
            pl.BlockSpec((tm, HALF), lambda i: (i, COL_GATE_A // HALF)),
            pl.BlockSpec((tm, HALF), lambda i: (i, COL_GATE_A // HALF + 1)),
            pl.BlockSpec((tm, HALF), lambda i: (i, COL_GATE_B // HALF)),
            pl.BlockSpec((tm, HALF), lambda i: (i, COL_GATE_B // HALF + 1)),
            pl.BlockSpec((D_MODEL, D_MODEL), const, pipeline_mode=pl.Buffered(1)),
            pl.BlockSpec((D_MODEL, D_MODEL), const, pipeline_mode=pl.Buffered(1)),
            pl.BlockSpec((D_MODEL, D_MODEL), const, pipeline_mode=pl.Buffered(1)),
            pl.BlockSpec((1, D_MODEL), const),
            pl.BlockSpec((1, D_MODEL), const),
        ],
        out_specs=pl.BlockSpec((tm, D_MODEL), row),
        out_shape=jax.ShapeDtypeStruct((m, D_MODEL), F32),
        scratch_shapes=[pltpu.VMEM((D_MODEL, D_MODEL), BF16)] * 3,
        compiler_params=pltpu.CompilerParams(
            dimension_semantics=("arbitrary",), vmem_limit_bytes=VMEM_LIMIT),
        name="merge_out",
    )(x2d, y_a, y_b, p_x2d, p_x2d, p_x2d, p_x2d, w_a, w_b, w_o, ln_w, ln_b)


def _one_layer(x, meta_tokens, w_in, b_gate, conv_w, a_log, dt_bias, gdn_norm_w, sinks,
               w_proj_a, w_proj_b, w_out, ln_w, ln_b):
    b, seq, _ = x.shape
    o_bd = 4 * GDN_W
    o_qb = o_bd + 2 * GDN_HEADS
    o_kb = o_qb + SWA_W
    o_zb = o_kb + 2 * SWA_KV_W
    w_segments = [w_in[:, :o_bd].astype(BF16), w_in[:, o_qb:].astype(BF16)]
    w_bd = jnp.pad(w_in[:, o_bd:o_qb], ((0, 0), (0, BD_W - 2 * GDN_HEADS))).astype(BF16)
    b_gate = b_gate.reshape(1, 2 * D_MODEL)

    x2d = x.reshape(b * seq, D_MODEL)
    meta_chunk = jnp.concatenate([jnp.zeros((META_PAD, D_MODEL), x.dtype), meta_tokens.astype(x.dtype)], axis=0)

    no_halo = jnp.zeros((CONV_HALO, CONV_W), F32)
    p_m, bd_m, meta_tail = _project(meta_chunk, w_segments, w_bd, conv_w, b_gate, no_halo,
                                    tm=CHUNK, tiles_per_seq=1)
    p_x2d, bd_x2d, _ = _project(x2d, w_segments, w_bd, conv_w, b_gate, meta_tail,
                                tm=PROJ_TM, tiles_per_seq=seq // PROJ_TM)
    p_x = p_x2d.reshape(b, seq, PROJ_W)
    bd_x = bd_x2d.reshape(b, seq, BD_W)

    y_a = _gdn(p_x, p_m, bd_x, bd_m, a_log.reshape(1, GDN_HEADS), dt_bias.reshape(1, GDN_HEADS),
               gdn_norm_w.reshape(1, GDN_D))
    y_b = _swa(p_x, p_m, sinks.reshape(1, SWA_HEADS))

    out = _output(x2d, y_a.reshape(b * seq, GDN_W), y_b.reshape(b * seq, SWA_W), p_x2d,
                  w_proj_a, w_proj_b, w_out,
                  ln_w.reshape(1, D_MODEL), ln_b.reshape(1, D_MODEL))
    return out.reshape(b, seq, D_MODEL)


def kernel(x, meta_tokens, w_in, b_gate, conv_w, a_log, dt_bias, gdn_norm_w, sinks,
           w_proj_a, w_proj_b, w_out, ln_w, ln_b):
    depth = w_in.shape[0]
    assert depth == 1, "meta tokens are projected once; deeper stacks need per-layer meta rows"
    return _one_layer(x, meta_tokens, w_in[0], b_gate[0], conv_w[0], a_log[0], dt_bias[0],
                      gdn_norm_w[0], sinks[0], w_proj_a[0], w_proj_b[0], w_out[0], ln_w[0], ln_b[0])
```

```python
import functools

import jax
import jax.numpy as jnp
from jax import lax
from jax.experimental import pallas as pl
from jax.experimental.pallas import tpu as pltpu

F32 = jnp.float32
BF16 = jnp.bfloat16

D_MODEL = 1024
CHUNK = 64
N_META = 16
META_PAD = CHUNK - N_META
GDN_HEADS = 8
GDN_D = 128
GDN_CONV = 4
GDN_W = GDN_HEADS * GDN_D
SWA_HEADS = 16
SWA_KV_HEADS = 4
SWA_GROUPS = SWA_HEADS // SWA_KV_HEADS
SWA_HD = 64
SWA_W = SWA_HEADS * SWA_HD
SWA_KV_W = SWA_KV_HEADS * SWA_HD
WINDOW_CHUNKS = 2
DEEPNORM_ALPHA = 2.0 ** 0.25
LN_EPS = 1e-5
RMS_EPS = 1e-6
L2_EPS = 1e-6
LOG2E = 1.4426950408889634

COL_Q_A = 0
COL_K_A = GDN_W
COL_V_A = 2 * GDN_W
COL_Z_A = 3 * GDN_W
COL_Q_B = 4 * GDN_W
COL_KV_B = COL_Q_B + SWA_W
COL_Z_B = COL_KV_B + 2 * SWA_KV_W
COL_GATE_A = COL_Z_B + SWA_W
COL_GATE_B = COL_GATE_A + D_MODEL
PROJ_W = COL_GATE_B + D_MODEL
HALF = 512
CONV_W = 3 * GDN_W
BD_W = 128
BD_COLS = 2 * GDN_HEADS

PROJ_TN = 256
PROJ_TM = 256
CONV_HALO = 8
VMEM_LIMIT = 48 * 1024 * 1024
PROJ_VMEM_LIMIT = 56 * 1024 * 1024


def _mm(a, b):
    return jnp.dot(a.astype(BF16), b.astype(BF16), preferred_element_type=F32)


def _mm_nt(a, b):
    return lax.dot_general(a.astype(BF16), b.astype(BF16), (((1,), (1,)), ((), ())),
                           preferred_element_type=F32)


def _mm_tn(a, b):
    return lax.dot_general(a.astype(BF16), b.astype(BF16), (((0,), (0,)), ((), ())),
                           preferred_element_type=F32)


def _mm_f32(a, b):
    return jnp.dot(a, b, preferred_element_type=F32, precision=lax.Precision.HIGHEST)


def _mm_tn_f32(a, b):
    return lax.dot_general(a, b, (((0,), (0,)), ((), ())), preferred_element_type=F32,
                           precision=lax.Precision.HIGHEST)


def _sigmoid(x):
    return 1.0 / (1.0 + jnp.exp2(x * -LOG2E))


SUBLANES = 8


def _shift_rows(a, k):
    rows, width = a.shape
    tiles = a.reshape(rows // SUBLANES, SUBLANES, width)
    rolled = pltpu.roll(tiles, k, axis=1)
    above = jnp.concatenate([rolled[:1], rolled[:-1]], axis=0)
    sub = lax.broadcasted_iota(jnp.int32, tiles.shape, 1)
    return jnp.where(sub < k, above, rolled).reshape(rows, width)


def _silu(x):
    return x * _sigmoid(x)


def _softplus(x):
    return jnp.maximum(x, 0.0) + jnp.log(1.0 + jnp.exp(-jnp.abs(x)))


def _proj_kernel(x_ref, w_ref, wbd_ref, convw_ref, bg_ref, halo0_ref,
                 p_ref, bd_ref, tail_ref, carry_ref, xb_ref, *, tm, tiles_per_seq):
    first_of_seq = (pl.program_id(0) % tiles_per_seq) == 0
    xb_ref[...] = x_ref[...].astype(BF16)
    bd_ref[...] = jnp.dot(xb_ref[...], wbd_ref[...], preferred_element_type=F32)

    n_tiles = PROJ_W // PROJ_TN
    heavy = list(range(CONV_W // PROJ_TN))
    light = list(range(CONV_W // PROJ_TN, n_tiles))
    order = []
    while heavy or light:
        order += heavy[:1] + light[:2]
        heavy, light = heavy[1:], light[2:]

    def matmul(j):
        return jnp.dot(xb_ref[...], w_ref[:, j * PROJ_TN:(j + 1) * PROJ_TN], preferred_element_type=F32)

    def epilogue(j, acc):
        c0 = j * PROJ_TN
        cols = slice(c0, c0 + PROJ_TN)
        if c0 < CONV_W:
            halo = jnp.where(first_of_seq, halo0_ref[:, cols], carry_ref[:, cols])
            tail = acc[tm - CONV_HALO:tm, :]
            carry_ref[:, cols] = tail
            tail_ref[:, cols] = tail
            raw = jnp.concatenate([halo, acc], axis=0)
            w0, w1, w2, w3 = (convw_ref[tap:tap + 1, cols] for tap in range(GDN_CONV))
            back1 = _shift_rows(raw, 1)
            u = raw * w1 + back1 * w0
            y = (acc * w3 + back1[CONV_HALO:, :] * w2) + _shift_rows(u, 2)[CONV_HALO:, :]
            y = _silu(y)
            if c0 < COL_V_A:
                post = GDN_D ** -0.5 if c0 < COL_K_A else 1.0
                groups = []
                for g in range(PROJ_TN // GDN_D):
                    yg = y[:, g * GDN_D:(g + 1) * GDN_D]
                    ss = jnp.sum(yg * yg, axis=-1, keepdims=True)
                    groups.append(yg * (lax.rsqrt(ss + L2_EPS) * post))
                y = jnp.concatenate(groups, axis=1)
        elif c0 < COL_Q_B or COL_Z_B <= c0 < COL_GATE_A:
            y = _silu(acc)
        elif c0 < COL_KV_B:
            y = acc * (LOG2E * SWA_HD ** -0.5)
        elif c0 < COL_Z_B:
            y = acc
        else:
            y = _sigmoid(acc + bg_ref[:, c0 - COL_GATE_A:c0 - COL_GATE_A + PROJ_TN])
        p_ref[:, cols] = y.astype(BF16)

    acc = matmul(order[0])
    for i, j in enumerate(order):
        nxt = matmul(order[i + 1]) if i + 1 < len(order) else None
        epilogue(j, acc)
        acc = nxt


def _weight_kernel(wt_ref, w_ref):
    w_ref[...] = wt_ref[...].T.astype(BF16)


def _projection_weight(w_in):
    first_q_b = COL_Q_B // PROJ_TN
    return pl.pallas_call(
        _weight_kernel,
        grid=(PROJ_W // PROJ_TN,),
        in_specs=[pl.BlockSpec((pl.Element(PROJ_TN), pl.Element(D_MODEL)),
                               lambda j: ((j * (PROJ_TN // BD_COLS) + jnp.where(j >= first_q_b, 1, 0)) * BD_COLS, 0))],
        out_specs=pl.BlockSpec((D_MODEL, PROJ_TN), lambda j: (0, j)),
        out_shape=jax.ShapeDtypeStruct((D_MODEL, PROJ_W), BF16),
        compiler_params=pltpu.CompilerParams(dimension_semantics=("arbitrary",)),
        name="weight_prep",
    )(jnp.swapaxes(w_in, 0, 1))


def _project(x2d, w_main, w_bd, conv_w, b_gate, halo0, tm, tiles_per_seq):
    m = x2d.shape[0]
    n_tiles = m // tm
    const = lambda i: (0, 0)
    resident = dict(pipeline_mode=pl.Buffered(1))
    return pl.pallas_call(
        functools.partial(_proj_kernel, tm=tm, tiles_per_seq=tiles_per_seq),
        grid=(n_tiles,),
        in_specs=[
            pl.BlockSpec((tm, D_MODEL), lambda i: (i, 0)),
            pl.BlockSpec((D_MODEL, PROJ_W), const, **resident),
            pl.BlockSpec((D_MODEL, BD_W), const, **resident),
            pl.BlockSpec((GDN_CONV, CONV_W), const, **resident),
            pl.BlockSpec((1, 2 * D_MODEL), const, **resident),
            pl.BlockSpec((CONV_HALO, CONV_W), const, **resident),
        ],
        out_specs=[
            pl.BlockSpec((tm, PROJ_W), lambda i: (i, 0)),
            pl.BlockSpec((tm, BD_W), lambda i: (i, 0)),
            pl.BlockSpec((CONV_HALO, CONV_W), lambda i: (i, 0)),
        ],
        out_shape=[
            jax.ShapeDtypeStruct((m, PROJ_W), BF16),
            jax.ShapeDtypeStruct((m, BD_W), F32),
            jax.ShapeDtypeStruct((n_tiles * CONV_HALO, CONV_W), F32),
        ],
        scratch_shapes=[pltpu.VMEM((CONV_HALO, CONV_W), F32), pltpu.VMEM((tm, D_MODEL), BF16)],
        compiler_params=pltpu.CompilerParams(
            dimension_semantics=("arbitrary",), vmem_limit_bytes=PROJ_VMEM_LIMIT),
        name="proj",
    )(x2d, w_main, w_bd, conv_w, b_gate, halo0)


GDN_SKEW = 2


GDN_PAIRS = GDN_HEADS // 2


def _gdn_consts(alog_ref, dtb_ref):
    ri = lax.broadcasted_iota(jnp.int32, (CHUNK, 2 * CHUNK), 0)
    lane = lax.broadcasted_iota(jnp.int32, (CHUNK, 2 * CHUNK), 1)
    ci = jnp.bitwise_and(lane, CHUNK - 1)
    levels = []
    s = 2
    while s < CHUNK:
        levels.append(((ri // (2 * s)) == (ci // (2 * s))) & ((ri // s) != (ci // s)))
        s *= 2
    r1 = lax.broadcasted_iota(jnp.int32, (CHUNK, CHUNK), 0)
    c1 = lax.broadcasted_iota(jnp.int32, (CHUNK, CHUNK), 1)
    return dict(tri=ri >= ci, strict=ri > ci, eye=(ri == ci).astype(F32),
                pair=(ri // 2) == (ci // 2), levels=levels, first=lane < CHUNK,
                lower_ones=(r1 >= c1).astype(F32), upper_ones=(r1 <= c1).astype(F32),
                neg_alpha=-jnp.exp(alog_ref[...]), dt_bias=dtb_ref[...])


def _block_diag(first, packed):
    zero = jnp.zeros_like(packed)
    return jnp.concatenate([jnp.where(first, packed, zero), jnp.where(first, zero, packed)], axis=0)


def _stack_diag(m0, m1):
    return jnp.concatenate([jnp.concatenate([m0, jnp.zeros_like(m1)], axis=1),
                            jnp.concatenate([jnp.zeros_like(m0), m1], axis=1)], axis=0)


def _gdn_chunk_stages(chunk, consts, result):
    qkv, bd, row_valid = chunk
    first = consts["first"]
    beta = _sigmoid(bd[:, 0:GDN_HEADS])
    g = consts["neg_alpha"] * _softplus(bd[:, GDN_HEADS:2 * GDN_HEADS] + consts["dt_bias"])
    if row_valid is not None:
        g = g * row_valid
    g_cum = _mm_f32(consts["lower_ones"], g)
    g_cum_t = _mm_tn_f32(g, consts["upper_ones"])
    yield

    heads = []
    for h in range(GDN_HEADS):
        q = qkv(h, 0).astype(F32)
        k_bf = qkv(h, 1)
        k = k_bf.astype(F32)
        v = qkv(h, 2).astype(F32)
        b_col = beta[:, h:h + 1]
        gc_col = g_cum[:, h:h + 1]
        gc_last = g_cum_t[h:h + 1, CHUNK - 1:CHUNK]
        e_col = jnp.exp(gc_col)
        k_beta = k * b_col
        heads.append(dict(
            k=k_bf, k_beta=k_beta.astype(BF16), q=qkv(h, 0),
            rhs=jnp.concatenate([v * b_col, k_beta * e_col], axis=1).astype(BF16),
            q_dec=(q * e_col).astype(BF16),
            k_dec=(k * jnp.exp(gc_last - gc_col)).astype(BF16),
            g_last=jnp.exp(gc_last)))
    pairs = []
    for p in range(GDN_PAIRS):
        h0, h1 = heads[2 * p], heads[2 * p + 1]
        gc_col = jnp.where(first, g_cum[:, 2 * p:2 * p + 1], g_cum[:, 2 * p + 1:2 * p + 2])
        gc_row = jnp.concatenate([g_cum_t[2 * p:2 * p + 1, :], g_cum_t[2 * p + 1:2 * p + 2, :]], axis=1)
        lhs = jnp.concatenate([jnp.concatenate([h0["k_beta"], h1["k_beta"]], axis=1),
                               jnp.concatenate([h0["q"], h1["q"]], axis=1)], axis=0)
        pairs.append(dict(
            decay=jnp.where(consts["tri"], jnp.exp(gc_col - gc_row), 0.0),
            kq=_mm_nt(lhs, _stack_diag(h0["k"], h1["k"]))))
    yield

    for pr in pairs:
        a = jnp.where(consts["strict"], pr["kq"][:CHUNK] * pr["decay"], 0.0)
        pr["a"] = _block_diag(first, a.astype(BF16))
        pr["attn"] = (pr["kq"][CHUNK:] * pr["decay"]).astype(BF16)
        pr["t"] = consts["eye"] - jnp.where(consts["pair"], a, 0.0)
    for off in consts["levels"]:
        for pr in pairs:
            pr["x"] = _mm(pr["t"], pr["a"])
        yield
        for pr in pairs:
            pr["x"] = _mm(pr["x"], _block_diag(first, pr["t"].astype(BF16)))
        yield
        for pr in pairs:
            pr["t"] = pr["t"] - jnp.where(off, pr["x"], 0.0)

    out = dict(u=[], w_q=[], k_dec=[], g_last=[], attn=[pr["attn"] for pr in pairs])
    for p, pr in enumerate(pairs):
        h0, h1 = heads[2 * p], heads[2 * p + 1]
        uw = _mm(pr["t"], _stack_diag(h0["rhs"], h1["rhs"]))
        for i, hd in enumerate((h0, h1)):
            base = 2 * i * GDN_D
            out["u"].append(uw[:, base:base + GDN_D])
            out["w_q"].append(jnp.concatenate([uw[:, base + GDN_D:base + 2 * GDN_D].astype(BF16), hd["q_dec"]],
                                              axis=0))
            out["k_dec"].append(hd["k_dec"])
            out["g_last"].append(hd["g_last"])
    result.update(out)


def _gdn_scan_stages(prep, state, result):
    r = [_mm(w_q, s) for w_q, s in zip(prep["w_q"], state)]
    yield
    v_new = [(u - r_h[:CHUNK]).astype(BF16) for u, r_h in zip(prep["u"], r)]
    av = []
    for p, attn in enumerate(prep["attn"]):
        both = _mm(attn, _stack_diag(v_new[2 * p], v_new[2 * p + 1]))
        av += [both[:, :GDN_D], both[:, GDN_D:]]
    ds = [_mm_tn(k_dec, vn) for k_dec, vn in zip(prep["k_dec"], v_new)]
    result["outs"] = [r_h[CHUNK:] + av_h for r_h, av_h in zip(r, av)]
    result["state"] = [s * g_last + ds_h for g_last, s, ds_h in zip(prep["g_last"], state, ds)]


def _gdn_run(chunks, state, alog_ref, dtb_ref, on_output):
    consts = _gdn_consts(alog_ref, dtb_ref)
    n = len(chunks)
    prepared = [{} for _ in range(n)]
    local = [_gdn_chunk_stages(chunks[c], consts, prepared[c]) for c in range(n)]
    running = [True] * n
    scan, scan_result, scan_chunk = None, None, 0
    slot = 0
    while scan_chunk < n:
        for c in range(n):
            if running[c] and slot >= GDN_SKEW * c:
                running[c] = next(local[c], "done") != "done"
        if scan is None and not running[scan_chunk]:
            scan_result = {}
            scan = _gdn_scan_stages(prepared[scan_chunk], state, scan_result)
        if scan is not None and next(scan, "done") == "done":
            state = scan_result["state"]
            on_output(scan_chunk, scan_result["outs"])
            scan, scan_chunk = None, scan_chunk + 1
        slot += 1
    return state


def _head_cols(part, h):
    col = part * GDN_W + h * GDN_D
    return slice(col, col + GDN_D)


def _gdn_meta_kernel(pm_ref, bdm_ref, alog_ref, dtb_ref, s0_ref):
    rows = lax.broadcasted_iota(jnp.int32, (CHUNK, 1), 0)
    row_valid = (rows >= META_PAD).astype(F32)
    qkv = lambda h, part: pm_ref[:, _head_cols(part, h)]
    state = _gdn_run([(qkv, bdm_ref[...], row_valid)], [jnp.zeros((GDN_D, GDN_D), F32)] * GDN_HEADS,
                     alog_ref, dtb_ref, lambda c, outs: None)
    for h in range(GDN_HEADS):
        s0_ref[h] = state[h]


def _gdn_kernel(pa_ref, bdx_ref, s0_ref, alog_ref, dtb_ref, nw_ref, y_ref, s_ref, *, chunks_per_tile):
    @pl.when(pl.program_id(1) == 0)
    def _():
        s_ref[...] = s0_ref[...]

    chunks = []
    for c in range(chunks_per_tile):
        rows = slice(c * CHUNK, (c + 1) * CHUNK)
        qkv = lambda h, part, rows=rows: pa_ref[0, rows, _head_cols(part, h)]
        chunks.append((qkv, bdx_ref[0, rows, :], None))

    def on_output(c, outs):
        rows = slice(c * CHUNK, (c + 1) * CHUNK)
        for h in range(GDN_HEADS):
            o = outs[h]
            o = o * lax.rsqrt(jnp.mean(o * o, axis=-1, keepdims=True) + RMS_EPS) * nw_ref[...]
            z = pa_ref[0, rows, _head_cols(3, h)].astype(F32)
            y_ref[0, rows, h * GDN_D:(h + 1) * GDN_D] = (o * z).astype(BF16)

    state = _gdn_run(chunks, [s_ref[h] for h in range(GDN_HEADS)], alog_ref, dtb_ref, on_output)
    for h in range(GDN_HEADS):
        s_ref[h] = state[h]


def _gdn(p_x, p_m, bd_x, bd_m, a_log, dt_bias, norm_w, chunks_per_tile=8):
    b, seq, _ = p_x.shape
    tile = chunks_per_tile * CHUNK
    const = lambda *_: (0, 0)
    params = pltpu.CompilerParams(dimension_semantics=("arbitrary", "arbitrary"),
                                  vmem_limit_bytes=VMEM_LIMIT)
    state_shape = (GDN_HEADS, GDN_D, GDN_D)

    s0 = pl.pallas_call(
        _gdn_meta_kernel,
        grid=(1, 1),
        in_specs=[
            pl.BlockSpec((CHUNK, CONV_W), const),
            pl.BlockSpec((CHUNK, BD_W), const),
            pl.BlockSpec((1, GDN_HEADS), const),
            pl.BlockSpec((1, GDN_HEADS), const),
        ],
        out_specs=pl.BlockSpec(state_shape, lambda *_: (0, 0, 0)),
        out_shape=jax.ShapeDtypeStruct(state_shape, F32),
        compiler_params=params,
        name="gdn_meta",
    )(p_m, bd_m, a_log, dt_bias)

    return pl.pallas_call(
        functools.partial(_gdn_kernel, chunks_per_tile=chunks_per_tile),
        grid=(b, seq // tile),
        in_specs=[
            pl.BlockSpec((1, tile, 4 * GDN_W), lambda i, t: (i, t, 0)),
            pl.BlockSpec((1, tile, BD_W), lambda i, t: (i, t, 0)),
            pl.BlockSpec(state_shape, lambda i, t: (0, 0, 0)),
            pl.BlockSpec((1, GDN_HEADS), const),
            pl.BlockSpec((1, GDN_HEADS), const),
            pl.BlockSpec((1, GDN_D), const),
        ],
        out_specs=pl.BlockSpec((1, tile, GDN_W), lambda i, t: (i, t, 0)),
        out_shape=jax.ShapeDtypeStruct((b, seq, GDN_W), BF16),
        scratch_shapes=[pltpu.VMEM(state_shape, F32)],
        compiler_params=params,
        name="gdn",
    )(p_x, bd_x, s0, a_log, dt_bias, norm_w)


SWA_BAND = (WINDOW_CHUNKS + 1) * CHUNK
SWA_KEYS = SWA_BAND + N_META
SWA_ROWS = SWA_GROUPS * CHUNK
LANES = 128
CHUNK_SHIFT = CHUNK.bit_length() - 1
SWA_GROUP_SIZE = 4


def _swa_tables(bias_ref, mterm_ref):
    qi = lax.broadcasted_iota(jnp.int32, (SWA_ROWS, SWA_KEYS), 0)
    kj = lax.broadcasted_iota(jnp.int32, (SWA_ROWS, SWA_KEYS), 1)
    q_in_chunk = jnp.bitwise_and(qi, CHUNK - 1)
    group = lax.shift_right_logical(qi, CHUNK_SHIFT)
    key_chunk = lax.shift_right_logical(kj, CHUNK_SHIFT)
    key_row = jnp.bitwise_and(kj, CHUNK - 1)
    is_meta = kj >= SWA_BAND
    for kvh in range(SWA_KV_HEADS):
        slope = jnp.zeros((SWA_ROWS, SWA_KEYS), F32)
        for g in range(SWA_GROUPS):
            head = kvh * SWA_GROUPS + g
            slope = jnp.where(group == g, LOG2E * 2.0 ** (-8.0 * (head + 1) / SWA_HEADS), slope)
        meta_bias = -slope * (q_in_chunk + N_META - (kj - SWA_BAND)).astype(F32)
        for e in range(WINDOW_CHUNKS + 1):
            d = (e - key_chunk) * CHUNK + q_in_chunk - key_row
            band_bias = jnp.where(key_chunk <= e, -slope * jnp.abs(d).astype(F32), -jnp.inf)
            bias_ref[e, kvh] = jnp.where(is_meta, meta_bias, band_bias)
        mterm_ref[kvh] = jnp.where(is_meta, slope * CHUNK, 0.0)[:, LANES:]


def _swa_kernel(q_ref, z0_ref, z1_ref, kv_ref, kvm_ref, sinks_ref, y_ref, bias_ref, mterm_ref, *,
                chunks_per_tile):
    t = pl.program_id(1)

    @pl.when((pl.program_id(0) == 0) & (t == 0))
    def _():
        _swa_tables(bias_ref, mterm_ref)

    group = lax.shift_right_logical(lax.broadcasted_iota(jnp.int32, (SWA_ROWS, 1), 0), CHUNK_SHIFT)
    sink_cols = []
    for kvh in range(SWA_KV_HEADS):
        sink = jnp.zeros((SWA_ROWS, 1), F32)
        for g in range(SWA_GROUPS):
            sink = jnp.where(group == g, sinks_ref[0, kvh * SWA_GROUPS + g] * LOG2E, sink)
        sink_cols.append(sink)
    kv_meta = kvm_ref[META_PAD:CHUNK, :]

    def scores(c, kvh):
        m = t * chunks_per_tile + c
        k0 = pl.multiple_of(jnp.maximum(m - WINDOW_CHUNKS, 0) * CHUNK, CHUNK)
        kv_win = kv_ref[0, pl.ds(k0, SWA_BAND), :]
        kcol = kvh * SWA_HD
        vcol = SWA_KV_W + kvh * SWA_HD
        k_win = jnp.concatenate([kv_win[:, kcol:kcol + SWA_HD], kv_meta[:, kcol:kcol + SWA_HD]], axis=0)
        v_win = jnp.concatenate([kv_win[:, vcol:vcol + SWA_HD], kv_meta[:, vcol:vcol + SWA_HD]], axis=0)
        rows = slice(c * CHUNK, (c + 1) * CHUNK)
        q_stack = jnp.concatenate(
            [q_ref[0, rows, (kvh * SWA_GROUPS + g) * SWA_HD:(kvh * SWA_GROUPS + g + 1) * SWA_HD]
             for g in range(SWA_GROUPS)], axis=0)
        return _mm_nt(q_stack, k_win), v_win, m

    def softmax(kvh, s, m):
        s = s + bias_ref[jnp.minimum(m, WINDOW_CHUNKS), kvh]
        s = jnp.concatenate([s[:, :LANES], s[:, LANES:] - mterm_ref[kvh] * m.astype(F32)], axis=1)
        sink = sink_cols[kvh]
        mx = jnp.maximum(jnp.max(s, axis=-1, keepdims=True), sink)
        return jnp.exp2(s - mx).astype(BF16), jnp.exp2(sink - mx)

    def store(c, kvh, o, key_sum, sink_term):
        o = o * (1.0 / (key_sum[:, :SWA_HD] + sink_term))
        o = jnp.concatenate([o[g * CHUNK:(g + 1) * CHUNK, :] for g in range(SWA_GROUPS)], axis=1)
        rows = slice(c * CHUNK, (c + 1) * CHUNK)
        cols = slice(kvh * SWA_GROUPS * SWA_HD, (kvh + 1) * SWA_GROUPS * SWA_HD)
        z_ref = (z0_ref, z1_ref)[cols.start // HALF]
        z = z_ref[0, rows, cols.start % HALF:cols.start % HALF + SWA_GROUPS * SWA_HD]
        y_ref[0, rows, cols] = (o * z.astype(F32)).astype(BF16)

    ones = jnp.ones((SWA_KEYS, LANES), BF16)

    def finish(group, scored):
        probs = [softmax(kvh, s, m) for (c, kvh), (s, _, m) in zip(group, scored)]
        outs = [_mm(e, v_win) for (e, _), (_, v_win, _) in zip(probs, scored)]
        sums = [_mm(e, ones) for e, _ in probs]
        for (c, kvh), o, key_sum, (_, sink_term) in zip(group, outs, sums, probs):
            store(c, kvh, o, key_sum, sink_term)

    order = [(c, kvh) for c in range(chunks_per_tile) for kvh in range(SWA_KV_HEADS)]
    groups = [order[i:i + SWA_GROUP_SIZE] for i in range(0, len(order), SWA_GROUP_SIZE)]
    scored = [scores(c, kvh) for c, kvh in groups[0]]
    for g, group in enumerate(groups):
        current = scored
        if g + 1 < len(groups):
            scored = [scores(c, kvh) for c, kvh in groups[g + 1]]
        finish(group, current)


def _swa(p_x, p_m, sinks, chunks_per_tile=4):
    b, seq, _ = p_x.shape
    tile = chunks_per_tile * CHUNK
    return pl.pallas_call(
        functools.partial(_swa_kernel, chunks_per_tile=chunks_per_tile),
        grid=(b, seq // tile),
        in_specs=[
            pl.BlockSpec((1, tile, SWA_W), lambda i, t: (i, t, COL_Q_B // SWA_W)),
            pl.BlockSpec((1, tile, HALF), lambda i, t: (i, t, COL_Z_B // HALF)),
            pl.BlockSpec((1, tile, HALF), lambda i, t: (i, t, COL_Z_B // HALF + 1)),
            pl.BlockSpec((1, seq, 2 * SWA_KV_W), lambda i, t: (i, 0, COL_KV_B // (2 * SWA_KV_W))),
            pl.BlockSpec((CHUNK, 2 * SWA_KV_W), lambda i, t: (0, COL_KV_B // (2 * SWA_KV_W))),
            pl.BlockSpec(memory_space=pltpu.SMEM),
        ],
        out_specs=pl.BlockSpec((1, tile, SWA_W), lambda i, t: (i, t, 0)),
        out_shape=jax.ShapeDtypeStruct((b, seq, SWA_W), BF16),
        scratch_shapes=[
            pltpu.VMEM((WINDOW_CHUNKS + 1, SWA_KV_HEADS, SWA_ROWS, SWA_KEYS), F32),
            pltpu.VMEM((SWA_KV_HEADS, SWA_ROWS, SWA_KEYS - LANES), F32),
        ],
        compiler_params=pltpu.CompilerParams(
            dimension_semantics=("arbitrary", "arbitrary"), vmem_limit_bytes=VMEM_LIMIT),
        name="swa",
    )(p_x, p_x, p_x, p_x, p_m, sinks)


def _out_kernel(x_ref, ya_ref, yb_ref, ga0_ref, ga1_ref, gb0_ref, gb1_ref, wa_ref, wb_ref, wo_ref,
                lnw_ref, lnb_ref, o_ref, wa_bf, wb_bf, wo_bf):
    @pl.when(pl.program_id(0) == 0)
    def _():
        wa_bf[...] = wa_ref[...].astype(BF16)
        wb_bf[...] = wb_ref[...].astype(BF16)
        wo_bf[...] = wo_ref[...].astype(BF16)

    g_a = jnp.concatenate([ga0_ref[...], ga1_ref[...]], axis=1).astype(F32)
    g_b = jnp.concatenate([gb0_ref[...], gb1_ref[...]], axis=1).astype(F32)
    mixed = g_a * _mm(ya_ref[...], wa_bf[...]) + g_b * _mm(yb_ref[...], wb_bf[...])
    r = DEEPNORM_ALPHA * x_ref[...] + _mm(mixed, wo_bf[...])
    mu = jnp.mean(r, axis=-1, keepdims=True)
    d = r - mu
    var = jnp.mean(d * d, axis=-1, keepdims=True)
    o_ref[...] = d * lax.rsqrt(var + LN_EPS) * lnw_ref[...] + lnb_ref[...]


def _output(x2d, y_a, y_b, p_x2d, w_a, w_b, w_o, ln_w, ln_b, tm=512):
    m = x2d.shape[0]
    row = lambda i: (i, 0)
    const = lambda i: (0, 0)
    return pl.pallas_call(
        _out_kernel,
        grid=(m // tm,),
        in_specs=[
            pl.BlockSpec((tm, D_MODEL), row),
            pl.BlockSpec((tm, D_MODEL), row),
            pl.BlockSpec((tm, D_MODEL), row),
            pl.BlockSpec((tm, HALF), lambda i: (i, COL_GATE_A // HALF)),
            pl.BlockSpec((tm, HALF), lambda i: (i, COL_GATE_A // HALF + 1)),
            pl.BlockSpec((tm, HALF), lambda i: (i, COL_GATE_B // HALF)),
            pl.BlockSpec((tm, HALF), lambda i: (i, COL_GATE_B // HALF + 1)),
            pl.BlockSpec((D_MODEL, D_MODEL), const, pipeline_mode=pl.Buffered(1)),
            pl.BlockSpec((D_MODEL, D_MODEL), const, pipeline_mode=pl.Buffered(1)),
            pl.BlockSpec((D_MODEL, D_MODEL), const, pipeline_mode=pl.Buffered(1)),
            pl.BlockSpec((1, D_MODEL), const),
            pl.BlockSpec((1, D_MODEL), const),
        ],
        out_specs=pl.BlockSpec((tm, D_MODEL), row),
        out_shape=jax.ShapeDtypeStruct((m, D_MODEL), F32),
        scratch_shapes=[pltpu.VMEM((D_MODEL, D_MODEL), BF16)] * 3,
        compiler_params=pltpu.CompilerParams(
            dimension_semantics=("arbitrary",), vmem_limit_bytes=VMEM_LIMIT),
        name="merge_out",
    )(x2d, y_a, y_b, p_x2d, p_x2d, p_x2d, p_x2d, w_a, w_b, w_o, ln_w, ln_b)


def _one_layer(x, meta_tokens, w_in, b_gate, conv_w, a_log, dt_bias, gdn_norm_w, sinks,
               w_proj_a, w_proj_b, w_out, ln_w, ln_b):
    b, seq, _ = x.shape
    o_bd = 4 * GDN_W
    o_qb = o_bd + 2 * GDN_HEADS
    o_kb = o_qb + SWA_W
    o_zb = o_kb + 2 * SWA_KV_W
    w_main = _projection_weight(w_in)
    w_bd = jnp.pad(w_in[:, o_bd:o_qb], ((0, 0), (0, BD_W - 2 * GDN_HEADS))).astype(BF16)
    b_gate = b_gate.reshape(1, 2 * D_MODEL)

    x2d = x.reshape(b * seq, D_MODEL)
    meta_chunk = jnp.concatenate([jnp.zeros((META_PAD, D_MODEL), x.dtype), meta_tokens.astype(x.dtype)], axis=0)

    no_halo = jnp.zeros((CONV_HALO, CONV_W), F32)
    p_m, bd_m, meta_tail = _project(meta_chunk, w_main, w_bd, conv_w, b_gate, no_halo,
                                    tm=CHUNK, tiles_per_seq=1)
    p_x2d, bd_x2d, _ = _project(x2d, w_main, w_bd, conv_w, b_gate, meta_tail,
                                tm=PROJ_TM, tiles_per_seq=seq // PROJ_TM)
    p_x = p_x2d.reshape(b, seq, PROJ_W)
    bd_x = bd_x2d.reshape(b, seq, BD_W)

    y_a = _gdn(p_x, p_m, bd_x, bd_m, a_log.reshape(1, GDN_HEADS), dt_bias.reshape(1, GDN_HEADS),
               gdn_norm_w.reshape(1, GDN_D))
    y_b = _swa(p_x, p_m, sinks.reshape(1, SWA_HEADS))

    out = _output(x2d, y_a.reshape(b * seq, GDN_W), y_b.reshape(b * seq, SWA_W), p_x2d,
                  w_proj_a, w_proj_b, w_out,
                  ln_w.reshape(1, D_MODEL), ln_b.reshape(1, D_MODEL))
    return out.reshape(b, seq, D_MODEL)


def kernel(x, meta_tokens, w_in, b_gate, conv_w, a_log, dt_bias, gdn_norm_w, sinks,
           w_proj_a, w_proj_b, w_out, ln_w, ln_b):
    depth = w_in.shape[0]
    assert depth == 1, "meta tokens are projected once; deeper stacks need per-layer meta rows"
    return _one_layer(x, meta_tokens, w_in[0], b_gate[0], conv_w[0], a_log[0], dt_bias[0],
                      gdn_norm_w[0], sinks[0], w_proj_a[0], w_proj_b[0], w_out[0], ln_w[0], ln_b[0])
```

```python
import functools

import jax
import jax.numpy as jnp
from jax import lax
from jax.experimental import pallas as pl
from jax.experimental.pallas import tpu as pltpu

F32 = jnp.float32
BF16 = jnp.bfloat16

D_MODEL = 1024
CHUNK = 64
N_META = 16
META_PAD = CHUNK - N_META
GDN_HEADS = 8
GDN_D = 128
GDN_CONV = 4
GDN_W = GDN_HEADS * GDN_D
SWA_HEADS = 16
SWA_KV_HEADS = 4
SWA_GROUPS = SWA_HEADS // SWA_KV_HEADS
SWA_HD = 64
SWA_W = SWA_HEADS * SWA_HD
SWA_KV_W = SWA_KV_HEADS * SWA_HD
WINDOW_CHUNKS = 2
DEEPNORM_ALPHA = 2.0 ** 0.25
LN_EPS = 1e-5
RMS_EPS = 1e-6
L2_EPS = 1e-6
LOG2E = 1.4426950408889634

COL_Q_A = 0
COL_K_A = GDN_W
COL_V_A = 2 * GDN_W
COL_Z_A = 3 * GDN_W
COL_Q_B = 4 * GDN_W
COL_KV_B = COL_Q_B + SWA_W
COL_Z_B = COL_KV_B + 2 * SWA_KV_W
COL_GATE_A = COL_Z_B + SWA_W
COL_GATE_B = COL_GATE_A + D_MODEL
PROJ_W = COL_GATE_B + D_MODEL
HALF = 512
CONV_W = 3 * GDN_W
BD_W = 128
BD_COLS = 2 * GDN_HEADS

PROJ_TN = 256
PROJ_TM = 256
CONV_HALO = 8
VMEM_LIMIT = 48 * 1024 * 1024
PROJ_VMEM_LIMIT = 56 * 1024 * 1024


def _mm(a, b):
    return jnp.dot(a.astype(BF16), b.astype(BF16), preferred_element_type=F32)


def _mm_nt(a, b):
    return lax.dot_general(a.astype(BF16), b.astype(BF16), (((1,), (1,)), ((), ())),
                           preferred_element_type=F32)


def _mm_tn(a, b):
    return lax.dot_general(a.astype(BF16), b.astype(BF16), (((0,), (0,)), ((), ())),
                           preferred_element_type=F32)


def _mm_f32(a, b):
    return jnp.dot(a, b, preferred_element_type=F32, precision=lax.Precision.HIGHEST)


def _mm_tn_f32(a, b):
    return lax.dot_general(a, b, (((0,), (0,)), ((), ())), preferred_element_type=F32,
                           precision=lax.Precision.HIGHEST)


def _sigmoid(x):
    return 1.0 / (1.0 + jnp.exp2(x * -LOG2E))


SUBLANES = 8


def _shift_rows(a, k):
    rows, width = a.shape
    tiles = a.reshape(rows // SUBLANES, SUBLANES, width)
    rolled = pltpu.roll(tiles, k, axis=1)
    above = jnp.concatenate([rolled[:1], rolled[:-1]], axis=0)
    sub = lax.broadcasted_iota(jnp.int32, tiles.shape, 1)
    return jnp.where(sub < k, above, rolled).reshape(rows, width)


def _silu(x):
    return x * _sigmoid(x)


def _softplus(x):
    return jnp.maximum(x, 0.0) + jnp.log(1.0 + jnp.exp(-jnp.abs(x)))


def _proj_kernel(x_ref, w_ref, wbd_ref, convw_ref, bg_ref, halo0_ref,
                 p_ref, bd_ref, tail_ref, carry_ref, xb_ref, *, tm, tiles_per_seq):
    first_of_seq = (pl.program_id(0) % tiles_per_seq) == 0
    xb_ref[...] = x_ref[...].astype(BF16)
    bd_ref[...] = jnp.dot(xb_ref[...], wbd_ref[...], preferred_element_type=F32)

    n_tiles = PROJ_W // PROJ_TN
    heavy = list(range(CONV_W // PROJ_TN))
    light = list(range(CONV_W // PROJ_TN, n_tiles))
    order = []
    while heavy or light:
        order += heavy[:1] + light[:2]
        heavy, light = heavy[1:], light[2:]

    def matmul(j):
        return jnp.dot(xb_ref[...], w_ref[:, j * PROJ_TN:(j + 1) * PROJ_TN], preferred_element_type=F32)

    def epilogue(j, acc):
        c0 = j * PROJ_TN
        cols = slice(c0, c0 + PROJ_TN)
        if c0 < CONV_W:
            halo = jnp.where(first_of_seq, halo0_ref[:, cols], carry_ref[:, cols])
            tail = acc[tm - CONV_HALO:tm, :]
            carry_ref[:, cols] = tail
            tail_ref[:, cols] = tail
            raw = jnp.concatenate([halo, acc], axis=0)
            w0, w1, w2, w3 = (convw_ref[tap:tap + 1, cols] for tap in range(GDN_CONV))
            back1 = _shift_rows(raw, 1)
            u = raw * w1 + back1 * w0
            y = (acc * w3 + back1[CONV_HALO:, :] * w2) + _shift_rows(u, 2)[CONV_HALO:, :]
            y = _silu(y)
            if c0 < COL_V_A:
                post = GDN_D ** -0.5 if c0 < COL_K_A else 1.0
                groups = []
                for g in range(PROJ_TN // GDN_D):
                    yg = y[:, g * GDN_D:(g + 1) * GDN_D]
                    ss = jnp.sum(yg * yg, axis=-1, keepdims=True)
                    groups.append(yg * (lax.rsqrt(ss + L2_EPS) * post))
                y = jnp.concatenate(groups, axis=1)
        elif c0 < COL_Q_B or COL_Z_B <= c0 < COL_GATE_A:
            y = _silu(acc)
        elif c0 < COL_KV_B:
            y = acc * (LOG2E * SWA_HD ** -0.5)
        elif c0 < COL_Z_B:
            y = acc
        else:
            y = _sigmoid(acc + bg_ref[:, c0 - COL_GATE_A:c0 - COL_GATE_A + PROJ_TN])
        p_ref[:, cols] = y.astype(BF16)

    acc = matmul(order[0])
    for i, j in enumerate(order):
        nxt = matmul(order[i + 1]) if i + 1 < len(order) else None
        epilogue(j, acc)
        acc = nxt


WEIGHT_TN = 512


def _weight_kernel(wt_ref, wt_bd_ref, w_ref, wbd_ref):
    w_ref[...] = wt_ref[...].T.astype(BF16)
    wbd_ref[...] = jnp.zeros_like(wbd_ref)
    wbd_ref[:, 0:BD_COLS] = wt_bd_ref[...].T.astype(BF16)


def _projection_weights(w_in):
    first_q_b = COL_Q_B // WEIGHT_TN
    return pl.pallas_call(
        _weight_kernel,
        grid=(PROJ_W // WEIGHT_TN,),
        in_specs=[
            pl.BlockSpec((pl.Element(WEIGHT_TN), pl.Element(D_MODEL)),
                         lambda j: ((j * (WEIGHT_TN // BD_COLS) + jnp.where(j >= first_q_b, 1, 0)) * BD_COLS, 0)),
            pl.BlockSpec((BD_COLS, D_MODEL), lambda j: (COL_Q_B // BD_COLS, 0)),
        ],
        out_specs=[
            pl.BlockSpec((D_MODEL, WEIGHT_TN), lambda j: (0, j)),
            pl.BlockSpec((D_MODEL, BD_W), lambda j: (0, 0)),
        ],
        out_shape=[
            jax.ShapeDtypeStruct((D_MODEL, PROJ_W), BF16),
            jax.ShapeDtypeStruct((D_MODEL, BD_W), BF16),
        ],
        compiler_params=pltpu.CompilerParams(dimension_semantics=("arbitrary",)),
        name="weight_prep",
    )(jnp.swapaxes(w_in, 0, 1), jnp.swapaxes(w_in, 0, 1))


def _project(x2d, w_main, w_bd, conv_w, b_gate, halo0, tm, tiles_per_seq):
    m = x2d.shape[0]
    n_tiles = m // tm
    const = lambda i: (0, 0)
    resident = dict(pipeline_mode=pl.Buffered(1))
    return pl.pallas_call(
        functools.partial(_proj_kernel, tm=tm, tiles_per_seq=tiles_per_seq),
        grid=(n_tiles,),
        in_specs=[
            pl.BlockSpec((tm, D_MODEL), lambda i: (i, 0)),
            pl.BlockSpec((D_MODEL, PROJ_W), const, **resident),
            pl.BlockSpec((D_MODEL, BD_W), const, **resident),
            pl.BlockSpec((GDN_CONV, CONV_W), const, **resident),
            pl.BlockSpec((1, 2 * D_MODEL), const, **resident),
            pl.BlockSpec((CONV_HALO, CONV_W), const, **resident),
        ],
        out_specs=[
            pl.BlockSpec((tm, PROJ_W), lambda i: (i, 0)),
            pl.BlockSpec((tm, BD_W), lambda i: (i, 0)),
            pl.BlockSpec((CONV_HALO, CONV_W), lambda i: (i, 0)),
        ],
        out_shape=[
            jax.ShapeDtypeStruct((m, PROJ_W), BF16),
            jax.ShapeDtypeStruct((m, BD_W), F32),
            jax.ShapeDtypeStruct((n_tiles * CONV_HALO, CONV_W), F32),
        ],
        scratch_shapes=[pltpu.VMEM((CONV_HALO, CONV_W), F32), pltpu.VMEM((tm, D_MODEL), BF16)],
        compiler_params=pltpu.CompilerParams(
            dimension_semantics=("arbitrary",), vmem_limit_bytes=PROJ_VMEM_LIMIT),
        name="proj",
    )(x2d, w_main, w_bd, conv_w, b_gate, halo0)


GDN_SKEW = 2


GDN_PAIRS = GDN_HEADS // 2


def _gdn_consts(alog_ref, dtb_ref):
    ri = lax.broadcasted_iota(jnp.int32, (CHUNK, 2 * CHUNK), 0)
    lane = lax.broadcasted_iota(jnp.int32, (CHUNK, 2 * CHUNK), 1)
    ci = jnp.bitwise_and(lane, CHUNK - 1)
    levels = []
    s = 2
    while s < CHUNK:
        levels.append(((ri // (2 * s)) == (ci // (2 * s))) & ((ri // s) != (ci // s)))
        s *= 2
    r1 = lax.broadcasted_iota(jnp.int32, (CHUNK, CHUNK), 0)
    c1 = lax.broadcasted_iota(jnp.int32, (CHUNK, CHUNK), 1)
    return dict(tri=ri >= ci, strict=ri > ci, eye=(ri == ci).astype(F32),
                pair=(ri // 2) == (ci // 2), levels=levels, first=lane < CHUNK,
                lower_ones=(r1 >= c1).astype(F32), upper_ones=(r1 <= c1).astype(F32),
                neg_alpha=-jnp.exp(alog_ref[...]), dt_bias=dtb_ref[...])


def _block_diag(first, packed):
    zero = jnp.zeros_like(packed)
    return jnp.concatenate([jnp.where(first, packed, zero), jnp.where(first, zero, packed)], axis=0)


def _stack_diag(m0, m1):
    return jnp.concatenate([jnp.concatenate([m0, jnp.zeros_like(m1)], axis=1),
                            jnp.concatenate([jnp.zeros_like(m0), m1], axis=1)], axis=0)


def _gdn_chunk_stages(chunk, consts, result):
    qkv, bd, row_valid = chunk
    first = consts["first"]
    beta = _sigmoid(bd[:, 0:GDN_HEADS])
    g = consts["neg_alpha"] * _softplus(bd[:, GDN_HEADS:2 * GDN_HEADS] + consts["dt_bias"])
    if row_valid is not None:
        g = g * row_valid
    g_cum = _mm_f32(consts["lower_ones"], g)
    g_cum_t = _mm_tn_f32(g, consts["upper_ones"])
    yield

    heads = []
    for h in range(GDN_HEADS):
        q = qkv(h, 0).astype(F32)
        k_bf = qkv(h, 1)
        k = k_bf.astype(F32)
        v = qkv(h, 2).astype(F32)
        b_col = beta[:, h:h + 1]
        gc_col = g_cum[:, h:h + 1]
        gc_last = g_cum_t[h:h + 1, CHUNK - 1:CHUNK]
        e_col = jnp.exp(gc_col)
        k_beta = k * b_col
        heads.append(dict(
            k=k_bf, k_beta=k_beta.astype(BF16), q=qkv(h, 0),
            rhs=jnp.concatenate([v * b_col, k_beta * e_col], axis=1).astype(BF16),
            q_dec=(q * e_col).astype(BF16),
            k_dec=(k * jnp.exp(gc_last - gc_col)).astype(BF16),
            g_last=jnp.exp(gc_last)))
    pairs = []
    for p in range(GDN_PAIRS):
        h0, h1 = heads[2 * p], heads[2 * p + 1]
        gc_col = jnp.where(first, g_cum[:, 2 * p:2 * p + 1], g_cum[:, 2 * p + 1:2 * p + 2])
        gc_row = jnp.concatenate([g_cum_t[2 * p:2 * p + 1, :], g_cum_t[2 * p + 1:2 * p + 2, :]], axis=1)
        lhs = jnp.concatenate([jnp.concatenate([h0["k_beta"], h1["k_beta"]], axis=1),
                               jnp.concatenate([h0["q"], h1["q"]], axis=1)], axis=0)
        pairs.append(dict(
            decay=jnp.where(consts["tri"], jnp.exp(gc_col - gc_row), 0.0),
            kq=_mm_nt(lhs, _stack_diag(h0["k"], h1["k"]))))
    yield

    for pr in pairs:
        a = jnp.where(consts["strict"], pr["kq"][:CHUNK] * pr["decay"], 0.0)
        pr["a"] = _block_diag(first, a.astype(BF16))
        pr["attn"] = (pr["kq"][CHUNK:] * pr["decay"]).astype(BF16)
        pr["t"] = consts["eye"] - jnp.where(consts["pair"], a, 0.0)
    for off in consts["levels"]:
        for pr in pairs:
            pr["x"] = _mm(pr["t"], pr["a"])
        yield
        for pr in pairs:
            pr["x"] = _mm(pr["x"], _block_diag(first, pr["t"].astype(BF16)))
        yield
        for pr in pairs:
            pr["t"] = pr["t"] - jnp.where(off, pr["x"], 0.0)

    out = dict(u=[], w_q=[], k_dec=[], g_last=[], attn=[pr["attn"] for pr in pairs])
    for p, pr in enumerate(pairs):
        h0, h1 = heads[2 * p], heads[2 * p + 1]
        uw = _mm(pr["t"], _stack_diag(h0["rhs"], h1["rhs"]))
        for i, hd in enumerate((h0, h1)):
            base = 2 * i * GDN_D
            out["u"].append(uw[:, base:base + GDN_D])
            out["w_q"].append(jnp.concatenate([uw[:, base + GDN_D:base + 2 * GDN_D].astype(BF16), hd["q_dec"]],
                                              axis=0))
            out["k_dec"].append(hd["k_dec"])
            out["g_last"].append(hd["g_last"])
    result.update(out)


def _gdn_scan_stages(prep, state, result):
    r = [_mm(w_q, s) for w_q, s in zip(prep["w_q"], state)]
    yield
    v_new = [(u - r_h[:CHUNK]).astype(BF16) for u, r_h in zip(prep["u"], r)]
    av = []
    for p, attn in enumerate(prep["attn"]):
        both = _mm(attn, _stack_diag(v_new[2 * p], v_new[2 * p + 1]))
        av += [both[:, :GDN_D], both[:, GDN_D:]]
    ds = [_mm_tn(k_dec, vn) for k_dec, vn in zip(prep["k_dec"], v_new)]
    result["outs"] = [r_h[CHUNK:] + av_h for r_h, av_h in zip(r, av)]
    result["state"] = [s * g_last + ds_h for g_last, s, ds_h in zip(prep["g_last"], state, ds)]


def _gdn_run(chunks, state, alog_ref, dtb_ref, on_output):
    consts = _gdn_consts(alog_ref, dtb_ref)
    n = len(chunks)
    prepared = [{} for _ in range(n)]
    local = [_gdn_chunk_stages(chunks[c], consts, prepared[c]) for c in range(n)]
    running = [True] * n
    scan, scan_result, scan_chunk = None, None, 0
    slot = 0
    while scan_chunk < n:
        for c in range(n):
            if running[c] and slot >= GDN_SKEW * c:
                running[c] = next(local[c], "done") != "done"
        if scan is None and not running[scan_chunk]:
            scan_result = {}
            scan = _gdn_scan_stages(prepared[scan_chunk], state, scan_result)
        if scan is not None and next(scan, "done") == "done":
            state = scan_result["state"]
            on_output(scan_chunk, scan_result["outs"])
            scan, scan_chunk = None, scan_chunk + 1
        slot += 1
    return state


def _head_cols(part, h):
    col = part * GDN_W + h * GDN_D
    return slice(col, col + GDN_D)


def _gdn_meta_kernel(pm_ref, bdm_ref, alog_ref, dtb_ref, s0_ref):
    rows = lax.broadcasted_iota(jnp.int32, (CHUNK, 1), 0)
    row_valid = (rows >= META_PAD).astype(F32)
    qkv = lambda h, part: pm_ref[:, _head_cols(part, h)]
    state = _gdn_run([(qkv, bdm_ref[...], row_valid)], [jnp.zeros((GDN_D, GDN_D), F32)] * GDN_HEADS,
                     alog_ref, dtb_ref, lambda c, outs: None)
    for h in range(GDN_HEADS):
        s0_ref[h] = state[h]


def _gdn_kernel(pa_ref, bdx_ref, s0_ref, alog_ref, dtb_ref, nw_ref, y_ref, s_ref, *, chunks_per_tile):
    @pl.when(pl.program_id(1) == 0)
    def _():
        s_ref[...] = s0_ref[...]

    chunks = []
    for c in range(chunks_per_tile):
        rows = slice(c * CHUNK, (c + 1) * CHUNK)
        qkv = lambda h, part, rows=rows: pa_ref[0, rows, _head_cols(part, h)]
        chunks.append((qkv, bdx_ref[0, rows, :], None))

    def on_output(c, outs):
        rows = slice(c * CHUNK, (c + 1) * CHUNK)
        for h in range(GDN_HEADS):
            o = outs[h]
            o = o * lax.rsqrt(jnp.mean(o * o, axis=-1, keepdims=True) + RMS_EPS) * nw_ref[...]
            z = pa_ref[0, rows, _head_cols(3, h)].astype(F32)
            y_ref[0, rows, h * GDN_D:(h + 1) * GDN_D] = (o * z).astype(BF16)

    state = _gdn_run(chunks, [s_ref[h] for h in range(GDN_HEADS)], alog_ref, dtb_ref, on_output)
    for h in range(GDN_HEADS):
        s_ref[h] = state[h]


def _gdn(p_x, p_m, bd_x, bd_m, a_log, dt_bias, norm_w, chunks_per_tile=8):
    b, seq, _ = p_x.shape
    tile = chunks_per_tile * CHUNK
    const = lambda *_: (0, 0)
    params = pltpu.CompilerParams(dimension_semantics=("arbitrary", "arbitrary"),
                                  vmem_limit_bytes=VMEM_LIMIT)
    state_shape = (GDN_HEADS, GDN_D, GDN_D)

    s0 = pl.pallas_call(
        _gdn_meta_kernel,
        grid=(1, 1),
        in_specs=[
            pl.BlockSpec((CHUNK, CONV_W), const),
            pl.BlockSpec((CHUNK, BD_W), const),
            pl.BlockSpec((1, GDN_HEADS), const),
            pl.BlockSpec((1, GDN_HEADS), const),
        ],
        out_specs=pl.BlockSpec(state_shape, lambda *_: (0, 0, 0)),
        out_shape=jax.ShapeDtypeStruct(state_shape, F32),
        compiler_params=params,
        name="gdn_meta",
    )(p_m, bd_m, a_log, dt_bias)

    return pl.pallas_call(
        functools.partial(_gdn_kernel, chunks_per_tile=chunks_per_tile),
        grid=(b, seq // tile),
        in_specs=[
            pl.BlockSpec((1, tile, 4 * GDN_W), lambda i, t: (i, t, 0)),
            pl.BlockSpec((1, tile, BD_W), lambda i, t: (i, t, 0)),
            pl.BlockSpec(state_shape, lambda i, t: (0, 0, 0)),
            pl.BlockSpec((1, GDN_HEADS), const),
            pl.BlockSpec((1, GDN_HEADS), const),
            pl.BlockSpec((1, GDN_D), const),
        ],
        out_specs=pl.BlockSpec((1, tile, GDN_W), lambda i, t: (i, t, 0)),
        out_shape=jax.ShapeDtypeStruct((b, seq, GDN_W), BF16),
        scratch_shapes=[pltpu.VMEM(state_shape, F32)],
        compiler_params=params,
        name="gdn",
    )(p_x, bd_x, s0, a_log, dt_bias, norm_w)


SWA_BAND = (WINDOW_CHUNKS + 1) * CHUNK
SWA_KEYS = SWA_BAND + N_META
SWA_ROWS = SWA_GROUPS * CHUNK
LANES = 128
CHUNK_SHIFT = CHUNK.bit_length() - 1
SWA_GROUP_SIZE = 4


def _swa_tables(bias_ref, mterm_ref):
    qi = lax.broadcasted_iota(jnp.int32, (SWA_ROWS, SWA_KEYS), 0)
    kj = lax.broadcasted_iota(jnp.int32, (SWA_ROWS, SWA_KEYS), 1)
    q_in_chunk = jnp.bitwise_and(qi, CHUNK - 1)
    group = lax.shift_right_logical(qi, CHUNK_SHIFT)
    key_chunk = lax.shift_right_logical(kj, CHUNK_SHIFT)
    key_row = jnp.bitwise_and(kj, CHUNK - 1)
    is_meta = kj >= SWA_BAND
    for kvh in range(SWA_KV_HEADS):
        slope = jnp.zeros((SWA_ROWS, SWA_KEYS), F32)
        for g in range(SWA_GROUPS):
            head = kvh * SWA_GROUPS + g
            slope = jnp.where(group == g, LOG2E * 2.0 ** (-8.0 * (head + 1) / SWA_HEADS), slope)
        meta_bias = -slope * (q_in_chunk + N_META - (kj - SWA_BAND)).astype(F32)
        for e in range(WINDOW_CHUNKS + 1):
            d = (e - key_chunk) * CHUNK + q_in_chunk - key_row
            band_bias = jnp.where(key_chunk <= e, -slope * jnp.abs(d).astype(F32), -jnp.inf)
            bias_ref[e, kvh] = jnp.where(is_meta, meta_bias, band_bias)
        mterm_ref[kvh] = jnp.where(is_meta, slope * CHUNK, 0.0)[:, LANES:]


def _swa_kernel(q_ref, z0_ref, z1_ref, kv_ref, kvm_ref, sinks_ref, y_ref, bias_ref, mterm_ref, *,
                chunks_per_tile):
    t = pl.program_id(1)

    @pl.when((pl.program_id(0) == 0) & (t == 0))
    def _():
        _swa_tables(bias_ref, mterm_ref)

    group = lax.shift_right_logical(lax.broadcasted_iota(jnp.int32, (SWA_ROWS, 1), 0), CHUNK_SHIFT)
    sink_cols = []
    for kvh in range(SWA_KV_HEADS):
        sink = jnp.zeros((SWA_ROWS, 1), F32)
        for g in range(SWA_GROUPS):
            sink = jnp.where(group == g, sinks_ref[0, kvh * SWA_GROUPS + g] * LOG2E, sink)
        sink_cols.append(sink)
    kv_meta = kvm_ref[META_PAD:CHUNK, :]

    def scores(c, kvh):
        m = t * chunks_per_tile + c
        k0 = pl.multiple_of(jnp.maximum(m - WINDOW_CHUNKS, 0) * CHUNK, CHUNK)
        kv_win = kv_ref[0, pl.ds(k0, SWA_BAND), :]
        kcol = kvh * SWA_HD
        vcol = SWA_KV_W + kvh * SWA_HD
        k_win = jnp.concatenate([kv_win[:, kcol:kcol + SWA_HD], kv_meta[:, kcol:kcol + SWA_HD]], axis=0)
        v_win = jnp.concatenate([kv_win[:, vcol:vcol + SWA_HD], kv_meta[:, vcol:vcol + SWA_HD]], axis=0)
        rows = slice(c * CHUNK, (c + 1) * CHUNK)
        q_stack = jnp.concatenate(
            [q_ref[0, rows, (kvh * SWA_GROUPS + g) * SWA_HD:(kvh * SWA_GROUPS + g + 1) * SWA_HD]
             for g in range(SWA_GROUPS)], axis=0)
        return _mm_nt(q_stack, k_win), v_win, m

    def softmax(kvh, s, m):
        s = s + bias_ref[jnp.minimum(m, WINDOW_CHUNKS), kvh]
        s = jnp.concatenate([s[:, :LANES], s[:, LANES:] - mterm_ref[kvh] * m.astype(F32)], axis=1)
        sink = sink_cols[kvh]
        mx = jnp.maximum(jnp.max(s, axis=-1, keepdims=True), sink)
        return jnp.exp2(s - mx).astype(BF16), jnp.exp2(sink - mx)

    def store(c, kvh, o, key_sum, sink_term):
        o = o * (1.0 / (key_sum[:, :SWA_HD] + sink_term))
        o = jnp.concatenate([o[g * CHUNK:(g + 1) * CHUNK, :] for g in range(SWA_GROUPS)], axis=1)
        rows = slice(c * CHUNK, (c + 1) * CHUNK)
        cols = slice(kvh * SWA_GROUPS * SWA_HD, (kvh + 1) * SWA_GROUPS * SWA_HD)
        z_ref = (z0_ref, z1_ref)[cols.start // HALF]
        z = z_ref[0, rows, cols.start % HALF:cols.start % HALF + SWA_GROUPS * SWA_HD]
        y_ref[0, rows, cols] = (o * z.astype(F32)).astype(BF16)

    ones = jnp.ones((SWA_KEYS, LANES), BF16)

    def finish(group, scored):
        probs = [softmax(kvh, s, m) for (c, kvh), (s, _, m) in zip(group, scored)]
        outs = [_mm(e, v_win) for (e, _), (_, v_win, _) in zip(probs, scored)]
        sums = [_mm(e, ones) for e, _ in probs]
        for (c, kvh), o, key_sum, (_, sink_term) in zip(group, outs, sums, probs):
            store(c, kvh, o, key_sum, sink_term)

    order = [(c, kvh) for c in range(chunks_per_tile) for kvh in range(SWA_KV_HEADS)]
    groups = [order[i:i + SWA_GROUP_SIZE] for i in range(0, len(order), SWA_GROUP_SIZE)]
    scored = [scores(c, kvh) for c, kvh in groups[0]]
    for g, group in enumerate(groups):
        current = scored
        if g + 1 < len(groups):
            scored = [scores(c, kvh) for c, kvh in groups[g + 1]]
        finish(group, current)


def _swa(p_x, p_m, sinks, chunks_per_tile=4):
    b, seq, _ = p_x.shape
    tile = chunks_per_tile * CHUNK
    return pl.pallas_call(
        functools.partial(_swa_kernel, chunks_per_tile=chunks_per_tile),
        grid=(b, seq // tile),
        in_specs=[
            pl.BlockSpec((1, tile, SWA_W), lambda i, t: (i, t, COL_Q_B // SWA_W)),
            pl.BlockSpec((1, tile, HALF), lambda i, t: (i, t, COL_Z_B // HALF)),
            pl.BlockSpec((1, tile, HALF), lambda i, t: (i, t, COL_Z_B // HALF + 1)),
            pl.BlockSpec((1, seq, 2 * SWA_KV_W), lambda i, t: (i, 0, COL_KV_B // (2 * SWA_KV_W))),
            pl.BlockSpec((CHUNK, 2 * SWA_KV_W), lambda i, t: (0, COL_KV_B // (2 * SWA_KV_W))),
            pl.BlockSpec(memory_space=pltpu.SMEM),
        ],
        out_specs=pl.BlockSpec((1, tile, SWA_W), lambda i, t: (i, t, 0)),
        out_shape=jax.ShapeDtypeStruct((b, seq, SWA_W), BF16),
        scratch_shapes=[
            pltpu.VMEM((WINDOW_CHUNKS + 1, SWA_KV_HEADS, SWA_ROWS, SWA_KEYS), F32),
            pltpu.VMEM((SWA_KV_HEADS, SWA_ROWS, SWA_KEYS - LANES), F32),
        ],
        compiler_params=pltpu.CompilerParams(
            dimension_semantics=("arbitrary", "arbitrary"), vmem_limit_bytes=VMEM_LIMIT),
        name="swa",
    )(p_x, p_x, p_x, p_x, p_m, sinks)


def _out_kernel(x_ref, ya_ref, yb_ref, ga0_ref, ga1_ref, gb0_ref, gb1_ref, wa_ref, wb_ref, wo_ref,
                lnw_ref, lnb_ref, o_ref, wa_bf, wb_bf, wo_bf):
    @pl.when(pl.program_id(0) == 0)
    def _():
        wa_bf[...] = wa_ref[...].astype(BF16)
        wb_bf[...] = wb_ref[...].astype(BF16)
        wo_bf[...] = wo_ref[...].astype(BF16)

    g_a = jnp.concatenate([ga0_ref[...], ga1_ref[...]], axis=1).astype(F32)
    g_b = jnp.concatenate([gb0_ref[...], gb1_ref[...]], axis=1).astype(F32)
    mixed = g_a * _mm(ya_ref[...], wa_bf[...]) + g_b * _mm(yb_ref[...], wb_bf[...])
    r = DEEPNORM_ALPHA * x_ref[...] + _mm(mixed, wo_bf[...])
    mu = jnp.mean(r, axis=-1, keepdims=True)
    d = r - mu
    var = jnp.mean(d * d, axis=-1, keepdims=True)
    o_ref[...] = d * lax.rsqrt(var + LN_EPS) * lnw_ref[...] + lnb_ref[...]


def _output(x2d, y_a, y_b, p_x2d, w_a, w_b, w_o, ln_w, ln_b, tm=512):
    m = x2d.shape[0]
    row = lambda i: (i, 0)
    const = lambda i: (0, 0)
    return pl.pallas_call(
        _out_kernel,
        grid=(m // tm,),
        in_specs=[
            pl.BlockSpec((tm, D_MODEL), row),
            pl.BlockSpec((tm, D_MODEL), row),
            pl.BlockSpec((tm, D_MODEL), row),
            pl.BlockSpec((tm, HALF), lambda i: (i, COL_GATE_A // HALF)),
            pl.BlockSpec((tm, HALF), lambda i: (i, COL_GATE_A // HALF + 1)),
            pl.BlockSpec((tm, HALF), lambda i: (i, COL_GATE_B // HALF)),
            pl.BlockSpec((tm, HALF), lambda i: (i, COL_GATE_B // HALF + 1)),
            pl.BlockSpec((D_MODEL, D_MODEL), const, pipeline_mode=pl.Buffered(1)),
            pl.BlockSpec((D_MODEL, D_MODEL), const, pipeline_mode=pl.Buffered(1)),
            pl.BlockSpec((D_MODEL, D_MODEL), const, pipeline_mode=pl.Buffered(1)),
            pl.BlockSpec((1, D_MODEL), const),
            pl.BlockSpec((1, D_MODEL), const),
        ],
        out_specs=pl.BlockSpec((tm, D_MODEL), row),
        out_shape=jax.ShapeDtypeStruct((m, D_MODEL), F32),
        scratch_shapes=[pltpu.VMEM((D_MODEL, D_MODEL), BF16)] * 3,
        compiler_params=pltpu.CompilerParams(
            dimension_semantics=("arbitrary",), vmem_limit_bytes=VMEM_LIMIT),
        name="merge_out",
    )(x2d, y_a, y_b, p_x2d, p_x2d, p_x2d, p_x2d, w_a, w_b, w_o, ln_w, ln_b)


def _one_layer(x, meta_tokens, w_in, b_gate, conv_w, a_log, dt_bias, gdn_norm_w, sinks,
               w_proj_a, w_proj_b, w_out, ln_w, ln_b):
    b, seq, _ = x.shape
    w_main, w_bd = _projection_weights(w_in)
    b_gate = b_gate.reshape(1, 2 * D_MODEL)

    x2d = x.reshape(b * seq, D_MODEL)
    meta_chunk = jnp.concatenate([jnp.zeros((META_PAD, D_MODEL), x.dtype), meta_tokens.astype(x.dtype)], axis=0)

    no_halo = jnp.zeros((CONV_HALO, CONV_W), F32)
    p_m, bd_m, meta_tail = _project(meta_chunk, w_main, w_bd, conv_w, b_gate, no_halo,
                                    tm=CHUNK, tiles_per_seq=1)
    p_x2d, bd_x2d, _ = _project(x2d, w_main, w_bd, conv_w, b_gate, meta_tail,
                                tm=PROJ_TM, tiles_per_seq=seq // PROJ_TM)
    p_x = p_x2d.reshape(b, seq, PROJ_W)
    bd_x = bd_x2d.reshape(b, seq, BD_W)

    y_a = _gdn(p_x, p_m, bd_x, bd_m, a_log.reshape(1, GDN_HEADS), dt_bias.reshape(1, GDN_HEADS),
               gdn_norm_w.reshape(1, GDN_D))
    y_b = _swa(p_x, p_m, sinks.reshape(1, SWA_HEADS))

    out = _output(x2d, y_a.reshape(b * seq, GDN_W), y_b.reshape(b * seq, SWA_W), p_x2d,
                  w_proj_a, w_proj_b, w_out,
                  ln_w.reshape(1, D_MODEL), ln_b.reshape(1, D_MODEL))
    return out.reshape(b, seq, D_MODEL)


def kernel(x, meta_tokens, w_in, b_gate, conv_w, a_log, dt_bias, gdn_norm_w, sinks,
           w_proj_a, w_proj_b, w_out, ln_w, ln_b):
    depth = w_in.shape[0]
    assert depth == 1, "meta tokens are projected once; deeper stacks need per-layer meta rows"
    return _one_layer(x, meta_tokens, w_in[0], b_gate[0], conv_w[0], a_log[0], dt_bias[0],
                      gdn_norm_w[0], sinks[0], w_proj_a[0], w_proj_b[0], w_out[0], ln_w[0], ln_b[0])
```

```python
import functools

import jax
import jax.numpy as jnp
from jax import lax
from jax.experimental import pallas as pl
from jax.experimental.pallas import tpu as pltpu

F32 = jnp.float32
BF16 = jnp.bfloat16

D_MODEL = 1024
CHUNK = 64
N_META = 16
META_PAD = CHUNK - N_META
GDN_HEADS = 8
GDN_D = 128
GDN_CONV = 4
GDN_W = GDN_HEADS * GDN_D
SWA_HEADS = 16
SWA_KV_HEADS = 4
SWA_GROUPS = SWA_HEADS // SWA_KV_HEADS
SWA_HD = 64
SWA_W = SWA_HEADS * SWA_HD
SWA_KV_W = SWA_KV_HEADS * SWA_HD
WINDOW_CHUNKS = 2
DEEPNORM_ALPHA = 2.0 ** 0.25
LN_EPS = 1e-5
RMS_EPS = 1e-6
L2_EPS = 1e-6
LOG2E = 1.4426950408889634

COL_Q_A = 0
COL_K_A = GDN_W
COL_V_A = 2 * GDN_W
COL_Z_A = 3 * GDN_W
COL_Q_B = 4 * GDN_W
COL_KV_B = COL_Q_B + SWA_W
COL_Z_B = COL_KV_B + 2 * SWA_KV_W
COL_GATE_A = COL_Z_B + SWA_W
COL_GATE_B = COL_GATE_A + D_MODEL
PROJ_W = COL_GATE_B + D_MODEL
HALF = 512
CONV_W = 3 * GDN_W
BD_W = 128
BD_COLS = 2 * GDN_HEADS

PROJ_TN = 256
PROJ_TM = 256
CONV_HALO = 8
VMEM_LIMIT = 48 * 1024 * 1024
PROJ_VMEM_LIMIT = 56 * 1024 * 1024


def _mm(a, b):
    return jnp.dot(a.astype(BF16), b.astype(BF16), preferred_element_type=F32)


def _mm_nt(a, b):
    return lax.dot_general(a.astype(BF16), b.astype(BF16), (((1,), (1,)), ((), ())),
                           preferred_element_type=F32)


def _mm_tn(a, b):
    return lax.dot_general(a.astype(BF16), b.astype(BF16), (((0,), (0,)), ((), ())),
                           preferred_element_type=F32)


def _mm_f32(a, b):
    return jnp.dot(a, b, preferred_element_type=F32, precision=lax.Precision.HIGHEST)


def _mm_tn_f32(a, b):
    return lax.dot_general(a, b, (((0,), (0,)), ((), ())), preferred_element_type=F32,
                           precision=lax.Precision.HIGHEST)


def _sigmoid(x):
    return 1.0 / (1.0 + jnp.exp2(x * -LOG2E))


SUBLANES = 8


def _shift_rows(a, k):
    rows, width = a.shape
    tiles = a.reshape(rows // SUBLANES, SUBLANES, width)
    rolled = pltpu.roll(tiles, k, axis=1)
    above = jnp.concatenate([rolled[:1], rolled[:-1]], axis=0)
    sub = lax.broadcasted_iota(jnp.int32, tiles.shape, 1)
    return jnp.where(sub < k, above, rolled).reshape(rows, width)


def _silu(x):
    return x * _sigmoid(x)


def _softplus(x):
    return jnp.maximum(x, 0.0) + jnp.log(1.0 + jnp.exp(-jnp.abs(x)))


def _proj_kernel(x_ref, w_ref, wbd_ref, convw_ref, bg_ref, halo0_ref,
                 p_ref, bd_ref, tail_ref, carry_ref, xb_ref, *, tm, tiles_per_seq):
    first_of_seq = (pl.program_id(0) % tiles_per_seq) == 0
    xb_ref[...] = x_ref[...].astype(BF16)
    bd_ref[...] = jnp.dot(xb_ref[...], wbd_ref[...], preferred_element_type=F32)

    n_tiles = PROJ_W // PROJ_TN
    heavy = list(range(CONV_W // PROJ_TN))
    light = list(range(CONV_W // PROJ_TN, n_tiles))
    order = []
    while heavy or light:
        order += heavy[:1] + light[:2]
        heavy, light = heavy[1:], light[2:]

    def matmul(j):
        return jnp.dot(xb_ref[...], w_ref[:, j * PROJ_TN:(j + 1) * PROJ_TN], preferred_element_type=F32)

    def epilogue(j, acc):
        c0 = j * PROJ_TN
        cols = slice(c0, c0 + PROJ_TN)
        if c0 < CONV_W:
            halo = jnp.where(first_of_seq, halo0_ref[:, cols], carry_ref[:, cols])
            tail = acc[tm - CONV_HALO:tm, :]
            carry_ref[:, cols] = tail
            tail_ref[:, cols] = tail
            raw = jnp.concatenate([halo, acc], axis=0)
            w0, w1, w2, w3 = (convw_ref[tap:tap + 1, cols] for tap in range(GDN_CONV))
            back1 = _shift_rows(raw, 1)
            u = raw * w1 + back1 * w0
            y = (acc * w3 + back1[CONV_HALO:, :] * w2) + _shift_rows(u, 2)[CONV_HALO:, :]
            y = _silu(y)
            if c0 < COL_V_A:
                post = GDN_D ** -0.5 if c0 < COL_K_A else 1.0
                groups = []
                for g in range(PROJ_TN // GDN_D):
                    yg = y[:, g * GDN_D:(g + 1) * GDN_D]
                    ss = jnp.sum(yg * yg, axis=-1, keepdims=True)
                    groups.append(yg * (lax.rsqrt(ss + L2_EPS) * post))
                y = jnp.concatenate(groups, axis=1)
        elif c0 < COL_Q_B or COL_Z_B <= c0 < COL_GATE_A:
            y = _silu(acc)
        elif c0 < COL_KV_B:
            y = acc * (LOG2E * SWA_HD ** -0.5)
        elif c0 < COL_Z_B:
            y = acc
        else:
            y = _sigmoid(acc + bg_ref[:, c0 - COL_GATE_A:c0 - COL_GATE_A + PROJ_TN])
        p_ref[:, cols] = y.astype(BF16)

    acc = matmul(order[0])
    for i, j in enumerate(order):
        nxt = matmul(order[i + 1]) if i + 1 < len(order) else None
        epilogue(j, acc)
        acc = nxt


WEIGHT_TN = 512


def _weight_kernel(wt_ref, wt_bd_ref, w_ref, wbd_ref):
    w_ref[...] = wt_ref[...].T.astype(BF16)
    wbd_ref[...] = jnp.zeros_like(wbd_ref)
    wbd_ref[:, 0:BD_COLS] = wt_bd_ref[...].T.astype(BF16)


def _projection_weights(w_in):
    first_q_b = COL_Q_B // WEIGHT_TN
    return pl.pallas_call(
        _weight_kernel,
        grid=(PROJ_W // WEIGHT_TN,),
        in_specs=[
            pl.BlockSpec((pl.Element(WEIGHT_TN), pl.Element(D_MODEL)),
                         lambda j: ((j * (WEIGHT_TN // BD_COLS) + jnp.where(j >= first_q_b, 1, 0)) * BD_COLS, 0)),
            pl.BlockSpec((BD_COLS, D_MODEL), lambda j: (COL_Q_B // BD_COLS, 0)),
        ],
        out_specs=[
            pl.BlockSpec((D_MODEL, WEIGHT_TN), lambda j: (0, j)),
            pl.BlockSpec((D_MODEL, BD_W), lambda j: (0, 0)),
        ],
        out_shape=[
            jax.ShapeDtypeStruct((D_MODEL, PROJ_W), BF16),
            jax.ShapeDtypeStruct((D_MODEL, BD_W), BF16),
        ],
        compiler_params=pltpu.CompilerParams(dimension_semantics=("arbitrary",)),
        name="weight_prep",
    )(jnp.swapaxes(w_in, 0, 1), jnp.swapaxes(w_in, 0, 1))


def _project(x2d, w_main, w_bd, conv_w, b_gate, halo0, tm, tiles_per_seq):
    m = x2d.shape[0]
    n_tiles = m // tm
    const = lambda i: (0, 0)
    resident = dict(pipeline_mode=pl.Buffered(1))
    return pl.pallas_call(
        functools.partial(_proj_kernel, tm=tm, tiles_per_seq=tiles_per_seq),
        grid=(n_tiles,),
        in_specs=[
            pl.BlockSpec((tm, D_MODEL), lambda i: (i, 0)),
            pl.BlockSpec((D_MODEL, PROJ_W), const, **resident),
            pl.BlockSpec((D_MODEL, BD_W), const, **resident),
            pl.BlockSpec((GDN_CONV, CONV_W), const, **resident),
            pl.BlockSpec((1, 2 * D_MODEL), const, **resident),
            pl.BlockSpec((CONV_HALO, CONV_W), const, **resident),
        ],
        out_specs=[
            pl.BlockSpec((tm, PROJ_W), lambda i: (i, 0)),
            pl.BlockSpec((tm, BD_W), lambda i: (i, 0)),
            pl.BlockSpec((CONV_HALO, CONV_W), lambda i: (i, 0)),
        ],
        out_shape=[
            jax.ShapeDtypeStruct((m, PROJ_W), BF16),
            jax.ShapeDtypeStruct((m, BD_W), F32),
            jax.ShapeDtypeStruct((n_tiles * CONV_HALO, CONV_W), F32),
        ],
        scratch_shapes=[pltpu.VMEM((CONV_HALO, CONV_W), F32), pltpu.VMEM((tm, D_MODEL), BF16)],
        compiler_params=pltpu.CompilerParams(
            dimension_semantics=("arbitrary",), vmem_limit_bytes=PROJ_VMEM_LIMIT),
        name="proj",
    )(x2d, w_main, w_bd, conv_w, b_gate, halo0)


GDN_SKEW = 2


GDN_PAIRS = GDN_HEADS // 2


def _gdn_consts(alog_ref, dtb_ref):
    ri = lax.broadcasted_iota(jnp.int32, (CHUNK, 2 * CHUNK), 0)
    lane = lax.broadcasted_iota(jnp.int32, (CHUNK, 2 * CHUNK), 1)
    ci = jnp.bitwise_and(lane, CHUNK - 1)
    levels = []
    s = 2
    while s < CHUNK:
        levels.append(((ri // (2 * s)) == (ci // (2 * s))) & ((ri // s) != (ci // s)))
        s *= 2
    r1 = lax.broadcasted_iota(jnp.int32, (CHUNK, CHUNK), 0)
    c1 = lax.broadcasted_iota(jnp.int32, (CHUNK, CHUNK), 1)
    return dict(tri=ri >= ci, strict=ri > ci, eye=(ri == ci).astype(F32),
                pair=(ri // 2) == (ci // 2), levels=levels, first=lane < CHUNK,
                lower_ones=(r1 >= c1).astype(F32), upper_ones=(r1 <= c1).astype(F32),
                neg_alpha=-jnp.exp(alog_ref[...]), dt_bias=dtb_ref[...])


def _block_diag(first, packed):
    zero = jnp.zeros_like(packed)
    return jnp.concatenate([jnp.where(first, packed, zero), jnp.where(first, zero, packed)], axis=0)


def _stack_diag(m0, m1):
    return jnp.concatenate([jnp.concatenate([m0, jnp.zeros_like(m1)], axis=1),
                            jnp.concatenate([jnp.zeros_like(m0), m1], axis=1)], axis=0)


def _gdn_chunk_stages(chunk, consts, result):
    qkv, bd, row_valid = chunk
    first = consts["first"]
    beta = _sigmoid(bd[:, 0:GDN_HEADS])
    g = consts["neg_alpha"] * _softplus(bd[:, GDN_HEADS:2 * GDN_HEADS] + consts["dt_bias"])
    if row_valid is not None:
        g = g * row_valid
    g_cum = _mm_f32(consts["lower_ones"], g)
    g_cum_t = _mm_tn_f32(g, consts["upper_ones"])
    yield

    heads = []
    for h in range(GDN_HEADS):
        q = qkv(h, 0).astype(F32)
        k_bf = qkv(h, 1)
        k = k_bf.astype(F32)
        v = qkv(h, 2).astype(F32)
        b_col = beta[:, h:h + 1]
        gc_col = g_cum[:, h:h + 1]
        gc_last = g_cum_t[h:h + 1, CHUNK - 1:CHUNK]
        e_col = jnp.exp(gc_col)
        k_beta = k * b_col
        heads.append(dict(
            k=k_bf, k_beta=k_beta.astype(BF16), q=qkv(h, 0),
            rhs=jnp.concatenate([v * b_col, k_beta * e_col], axis=1).astype(BF16),
            q_dec=(q * e_col).astype(BF16),
            k_dec=(k * jnp.exp(gc_last - gc_col)).astype(BF16),
            g_last=jnp.exp(gc_last)))
    pairs = []
    for p in range(GDN_PAIRS):
        h0, h1 = heads[2 * p], heads[2 * p + 1]
        gc_col = jnp.where(first, g_cum[:, 2 * p:2 * p + 1], g_cum[:, 2 * p + 1:2 * p + 2])
        gc_row = jnp.concatenate([g_cum_t[2 * p:2 * p + 1, :], g_cum_t[2 * p + 1:2 * p + 2, :]], axis=1)
        lhs = jnp.concatenate([jnp.concatenate([h0["k_beta"], h1["k_beta"]], axis=1),
                               jnp.concatenate([h0["q"], h1["q"]], axis=1)], axis=0)
        pairs.append(dict(
            decay=jnp.where(consts["tri"], jnp.exp(gc_col - gc_row), 0.0),
            kq=_mm_nt(lhs, _stack_diag(h0["k"], h1["k"]))))
    yield

    for pr in pairs:
        a = jnp.where(consts["strict"], pr["kq"][:CHUNK] * pr["decay"], 0.0)
        pr["a"] = _block_diag(first, a.astype(BF16))
        pr["attn"] = (pr["kq"][CHUNK:] * pr["decay"]).astype(BF16)
        pr["t"] = consts["eye"] - jnp.where(consts["pair"], a, 0.0)
    for off in consts["levels"]:
        for pr in pairs:
            pr["x"] = _mm(pr["t"], pr["a"])
        yield
        for pr in pairs:
            pr["x"] = _mm(pr["x"], _block_diag(first, pr["t"].astype(BF16)))
        yield
        for pr in pairs:
            pr["t"] = pr["t"] - jnp.where(off, pr["x"], 0.0)

    out = dict(u=[], w_q=[], k_dec=[], g_last=[], attn=[pr["attn"] for pr in pairs])
    for p, pr in enumerate(pairs):
        h0, h1 = heads[2 * p], heads[2 * p + 1]
        uw = _mm(pr["t"], _stack_diag(h0["rhs"], h1["rhs"]))
        for i, hd in enumerate((h0, h1)):
            base = 2 * i * GDN_D
            out["u"].append(uw[:, base:base + GDN_D])
            out["w_q"].append(jnp.concatenate([uw[:, base + GDN_D:base + 2 * GDN_D].astype(BF16), hd["q_dec"]],
                                              axis=0))
            out["k_dec"].append(hd["k_dec"])
            out["g_last"].append(hd["g_last"])
    result.update(out)


def _gdn_scan_stages(prep, state, result):
    r = [_mm(w_q, s) for w_q, s in zip(prep["w_q"], state)]
    yield
    v_new = [(u - r_h[:CHUNK]).astype(BF16) for u, r_h in zip(prep["u"], r)]
    av = []
    for p, attn in enumerate(prep["attn"]):
        both = _mm(attn, _stack_diag(v_new[2 * p], v_new[2 * p + 1]))
        av += [both[:, :GDN_D], both[:, GDN_D:]]
    ds = [_mm_tn(k_dec, vn) for k_dec, vn in zip(prep["k_dec"], v_new)]
    result["outs"] = [r_h[CHUNK:] + av_h for r_h, av_h in zip(r, av)]
    result["state"] = [s * g_last + ds_h for g_last, s, ds_h in zip(prep["g_last"], state, ds)]


class _GdnSequence:
    def __init__(self, chunks, state, consts):
        self.n = len(chunks)
        self.state = state
        self.prepared = [{} for _ in range(self.n)]
        self.local = [_gdn_chunk_stages(chunks[c], consts, self.prepared[c]) for c in range(self.n)]
        self.running = [True] * self.n
        self.scan, self.scan_result, self.scan_chunk = None, None, 0

    def advance_local(self, slot):
        for c in range(self.n):
            if self.running[c] and slot >= GDN_SKEW * c:
                self.running[c] = next(self.local[c], "done") != "done"

    def advance_scan(self, on_output):
        if self.scan_chunk == self.n:
            return
        if self.scan is None and not self.running[self.scan_chunk]:
            self.scan_result = {}
            self.scan = _gdn_scan_stages(self.prepared[self.scan_chunk], self.state, self.scan_result)
        if self.scan is not None and next(self.scan, "done") == "done":
            self.state = self.scan_result["state"]
            on_output(self.scan_chunk, self.scan_result["outs"])
            self.scan, self.scan_chunk = None, self.scan_chunk + 1


def _gdn_run(sequences, states, alog_ref, dtb_ref, on_output):
    consts = _gdn_consts(alog_ref, dtb_ref)
    seqs = [_GdnSequence(chunks, state, consts) for chunks, state in zip(sequences, states)]
    slot = 0
    while any(s.scan_chunk < s.n for s in seqs):
        for s in seqs:
            s.advance_local(slot)
        for i, s in enumerate(seqs):
            s.advance_scan(functools.partial(on_output, i))
        slot += 1
    return [s.state for s in seqs]


def _head_cols(part, h):
    col = part * GDN_W + h * GDN_D
    return slice(col, col + GDN_D)


def _gdn_meta_kernel(pm_ref, bdm_ref, alog_ref, dtb_ref, s0_ref):
    rows = lax.broadcasted_iota(jnp.int32, (CHUNK, 1), 0)
    row_valid = (rows >= META_PAD).astype(F32)
    qkv = lambda h, part: pm_ref[:, _head_cols(part, h)]
    (state,) = _gdn_run([[(qkv, bdm_ref[...], row_valid)]], [[jnp.zeros((GDN_D, GDN_D), F32)] * GDN_HEADS],
                        alog_ref, dtb_ref, lambda s, c, outs: None)
    for h in range(GDN_HEADS):
        s0_ref[h] = state[h]


def _gdn_kernel(pa_ref, bdx_ref, s0_ref, alog_ref, dtb_ref, nw_ref, y_ref, s_ref, *, chunks_per_tile, seqs):
    @pl.when(pl.program_id(1) == 0)
    def _():
        for s in range(seqs):
            s_ref[s] = s0_ref[...]

    sequences = []
    for s in range(seqs):
        chunks = []
        for c in range(chunks_per_tile):
            rows = slice(c * CHUNK, (c + 1) * CHUNK)
            qkv = lambda h, part, s=s, rows=rows: pa_ref[s, rows, _head_cols(part, h)]
            chunks.append((qkv, bdx_ref[s, rows, :], None))
        sequences.append(chunks)

    def on_output(s, c, outs):
        rows = slice(c * CHUNK, (c + 1) * CHUNK)
        for h in range(GDN_HEADS):
            o = outs[h]
            o = o * lax.rsqrt(jnp.mean(o * o, axis=-1, keepdims=True) + RMS_EPS) * nw_ref[...]
            z = pa_ref[s, rows, _head_cols(3, h)].astype(F32)
            y_ref[s, rows, h * GDN_D:(h + 1) * GDN_D] = (o * z).astype(BF16)

    states = _gdn_run(sequences, [[s_ref[s, h] for h in range(GDN_HEADS)] for s in range(seqs)],
                      alog_ref, dtb_ref, on_output)
    for s in range(seqs):
        for h in range(GDN_HEADS):
            s_ref[s, h] = states[s][h]


def _gdn(p_x, p_m, bd_x, bd_m, a_log, dt_bias, norm_w, chunks_per_tile=4, seqs=2):
    b, seq, _ = p_x.shape
    tile = chunks_per_tile * CHUNK
    const = lambda *_: (0, 0)
    params = pltpu.CompilerParams(dimension_semantics=("arbitrary", "arbitrary"),
                                  vmem_limit_bytes=VMEM_LIMIT)
    state_shape = (GDN_HEADS, GDN_D, GDN_D)

    s0 = pl.pallas_call(
        _gdn_meta_kernel,
        grid=(1, 1),
        in_specs=[
            pl.BlockSpec((CHUNK, CONV_W), const),
            pl.BlockSpec((CHUNK, BD_W), const),
            pl.BlockSpec((1, GDN_HEADS), const),
            pl.BlockSpec((1, GDN_HEADS), const),
        ],
        out_specs=pl.BlockSpec(state_shape, lambda *_: (0, 0, 0)),
        out_shape=jax.ShapeDtypeStruct(state_shape, F32),
        compiler_params=params,
        name="gdn_meta",
    )(p_m, bd_m, a_log, dt_bias)

    return pl.pallas_call(
        functools.partial(_gdn_kernel, chunks_per_tile=chunks_per_tile, seqs=seqs),
        grid=(b // seqs, seq // tile),
        in_specs=[
            pl.BlockSpec((seqs, tile, 4 * GDN_W), lambda i, t: (i, t, 0)),
            pl.BlockSpec((seqs, tile, BD_W), lambda i, t: (i, t, 0)),
            pl.BlockSpec(state_shape, lambda i, t: (0, 0, 0)),
            pl.BlockSpec((1, GDN_HEADS), const),
            pl.BlockSpec((1, GDN_HEADS), const),
            pl.BlockSpec((1, GDN_D), const),
        ],
        out_specs=pl.BlockSpec((seqs, tile, GDN_W), lambda i, t: (i, t, 0)),
        out_shape=jax.ShapeDtypeStruct((b, seq, GDN_W), BF16),
        scratch_shapes=[pltpu.VMEM((seqs,) + state_shape, F32)],
        compiler_params=params,
        name="gdn",
    )(p_x, bd_x, s0, a_log, dt_bias, norm_w)


SWA_BAND = (WINDOW_CHUNKS + 1) * CHUNK
SWA_KEYS = SWA_BAND + N_META
SWA_ROWS = SWA_GROUPS * CHUNK
LANES = 128
CHUNK_SHIFT = CHUNK.bit_length() - 1
SWA_GROUP_SIZE = 4


def _swa_tables(bias_ref, mterm_ref):
    qi = lax.broadcasted_iota(jnp.int32, (SWA_ROWS, SWA_KEYS), 0)
    kj = lax.broadcasted_iota(jnp.int32, (SWA_ROWS, SWA_KEYS), 1)
    q_in_chunk = jnp.bitwise_and(qi, CHUNK - 1)
    group = lax.shift_right_logical(qi, CHUNK_SHIFT)
    key_chunk = lax.shift_right_logical(kj, CHUNK_SHIFT)
    key_row = jnp.bitwise_and(kj, CHUNK - 1)
    is_meta = kj >= SWA_BAND
    for kvh in range(SWA_KV_HEADS):
        slope = jnp.zeros((SWA_ROWS, SWA_KEYS), F32)
        for g in range(SWA_GROUPS):
            head = kvh * SWA_GROUPS + g
            slope = jnp.where(group == g, LOG2E * 2.0 ** (-8.0 * (head + 1) / SWA_HEADS), slope)
        meta_bias = -slope * (q_in_chunk + N_META - (kj - SWA_BAND)).astype(F32)
        for e in range(WINDOW_CHUNKS + 1):
            d = (e - key_chunk) * CHUNK + q_in_chunk - key_row
            band_bias = jnp.where(key_chunk <= e, -slope * jnp.abs(d).astype(F32), -jnp.inf)
            bias_ref[e, kvh] = jnp.where(is_meta, meta_bias, band_bias)
        mterm_ref[kvh] = jnp.where(is_meta, slope * CHUNK, 0.0)[:, LANES:]


def _swa_kernel(q_ref, z0_ref, z1_ref, kv_ref, kvm_ref, sinks_ref, y_ref, bias_ref, mterm_ref, *,
                chunks_per_tile):
    t = pl.program_id(1)

    @pl.when((pl.program_id(0) == 0) & (t == 0))
    def _():
        _swa_tables(bias_ref, mterm_ref)

    group = lax.shift_right_logical(lax.broadcasted_iota(jnp.int32, (SWA_ROWS, 1), 0), CHUNK_SHIFT)
    sink_cols = []
    for kvh in range(SWA_KV_HEADS):
        sink = jnp.zeros((SWA_ROWS, 1), F32)
        for g in range(SWA_GROUPS):
            sink = jnp.where(group == g, sinks_ref[0, kvh * SWA_GROUPS + g] * LOG2E, sink)
        sink_cols.append(sink)
    kv_meta = kvm_ref[META_PAD:CHUNK, :]

    def scores(c, kvh):
        m = t * chunks_per_tile + c
        k0 = pl.multiple_of(jnp.maximum(m - WINDOW_CHUNKS, 0) * CHUNK, CHUNK)
        kv_win = kv_ref[0, pl.ds(k0, SWA_BAND), :]
        kcol = kvh * SWA_HD
        vcol = SWA_KV_W + kvh * SWA_HD
        k_win = jnp.concatenate([kv_win[:, kcol:kcol + SWA_HD], kv_meta[:, kcol:kcol + SWA_HD]], axis=0)
        v_win = jnp.concatenate([kv_win[:, vcol:vcol + SWA_HD], kv_meta[:, vcol:vcol + SWA_HD]], axis=0)
        rows = slice(c * CHUNK, (c + 1) * CHUNK)
        q_stack = jnp.concatenate(
            [q_ref[0, rows, (kvh * SWA_GROUPS + g) * SWA_HD:(kvh * SWA_GROUPS + g + 1) * SWA_HD]
             for g in range(SWA_GROUPS)], axis=0)
        return _mm_nt(q_stack, k_win), v_win, m

    def softmax(kvh, s, m):
        s = s + bias_ref[jnp.minimum(m, WINDOW_CHUNKS), kvh]
        s = jnp.concatenate([s[:, :LANES], s[:, LANES:] - mterm_ref[kvh] * m.astype(F32)], axis=1)
        sink = sink_cols[kvh]
        mx = jnp.maximum(jnp.max(s, axis=-1, keepdims=True), sink)
        return jnp.exp2(s - mx).astype(BF16), jnp.exp2(sink - mx)

    def store(c, kvh, o, key_sum, sink_term):
        o = o * (1.0 / (key_sum[:, :SWA_HD] + sink_term))
        o = jnp.concatenate([o[g * CHUNK:(g + 1) * CHUNK, :] for g in range(SWA_GROUPS)], axis=1)
        rows = slice(c * CHUNK, (c + 1) * CHUNK)
        cols = slice(kvh * SWA_GROUPS * SWA_HD, (kvh + 1) * SWA_GROUPS * SWA_HD)
        z_ref = (z0_ref, z1_ref)[cols.start // HALF]
        z = z_ref[0, rows, cols.start % HALF:cols.start % HALF + SWA_GROUPS * SWA_HD]
        y_ref[0, rows, cols] = (o * z.astype(F32)).astype(BF16)

    ones = jnp.ones((SWA_KEYS, LANES), BF16)

    def finish(group, scored):
        probs = [softmax(kvh, s, m) for (c, kvh), (s, _, m) in zip(group, scored)]
        outs = [_mm(e, v_win) for (e, _), (_, v_win, _) in zip(probs, scored)]
        sums = [_mm(e, ones) for e, _ in probs]
        for (c, kvh), o, key_sum, (_, sink_term) in zip(group, outs, sums, probs):
            store(c, kvh, o, key_sum, sink_term)

    order = [(c, kvh) for c in range(chunks_per_tile) for kvh in range(SWA_KV_HEADS)]
    groups = [order[i:i + SWA_GROUP_SIZE] for i in range(0, len(order), SWA_GROUP_SIZE)]
    scored = [scores(c, kvh) for c, kvh in groups[0]]
    for g, group in enumerate(groups):
        current = scored
        if g + 1 < len(groups):
            scored = [scores(c, kvh) for c, kvh in groups[g + 1]]
        finish(group, current)


def _swa(p_x, p_m, sinks, chunks_per_tile=4):
    b, seq, _ = p_x.shape
    tile = chunks_per_tile * CHUNK
    return pl.pallas_call(
        functools.partial(_swa_kernel, chunks_per_tile=chunks_per_tile),
        grid=(b, seq // tile),
        in_specs=[
            pl.BlockSpec((1, tile, SWA_W), lambda i, t: (i, t, COL_Q_B // SWA_W)),
            pl.BlockSpec((1, tile, HALF), lambda i, t: (i, t, COL_Z_B // HALF)),
            pl.BlockSpec((1, tile, HALF), lambda i, t: (i, t, COL_Z_B // HALF + 1)),
            pl.BlockSpec((1, seq, 2 * SWA_KV_W), lambda i, t: (i, 0, COL_KV_B // (2 * SWA_KV_W))),
            pl.BlockSpec((CHUNK, 2 * SWA_KV_W), lambda i, t: (0, COL_KV_B // (2 * SWA_KV_W))),
            pl.BlockSpec(memory_space=pltpu.SMEM),
        ],
        out_specs=pl.BlockSpec((1, tile, SWA_W), lambda i, t: (i, t, 0)),
        out_shape=jax.ShapeDtypeStruct((b, seq, SWA_W), BF16),
        scratch_shapes=[
            pltpu.VMEM((WINDOW_CHUNKS + 1, SWA_KV_HEADS, SWA_ROWS, SWA_KEYS), F32),
            pltpu.VMEM((SWA_KV_HEADS, SWA_ROWS, SWA_KEYS - LANES), F32),
        ],
        compiler_params=pltpu.CompilerParams(
            dimension_semantics=("arbitrary", "arbitrary"), vmem_limit_bytes=VMEM_LIMIT),
        name="swa",
    )(p_x, p_x, p_x, p_x, p_m, sinks)


def _out_kernel(x_ref, ya_ref, yb_ref, ga0_ref, ga1_ref, gb0_ref, gb1_ref, wa_ref, wb_ref, wo_ref,
                lnw_ref, lnb_ref, o_ref, wa_bf, wb_bf, wo_bf):
    @pl.when(pl.program_id(0) == 0)
    def _():
        wa_bf[...] = wa_ref[...].astype(BF16)
        wb_bf[...] = wb_ref[...].astype(BF16)
        wo_bf[...] = wo_ref[...].astype(BF16)

    g_a = jnp.concatenate([ga0_ref[...], ga1_ref[...]], axis=1).astype(F32)
    g_b = jnp.concatenate([gb0_ref[...], gb1_ref[...]], axis=1).astype(F32)
    mixed = g_a * _mm(ya_ref[...], wa_bf[...]) + g_b * _mm(yb_ref[...], wb_bf[...])
    r = DEEPNORM_ALPHA * x_ref[...] + _mm(mixed, wo_bf[...])
    mu = jnp.mean(r, axis=-1, keepdims=True)
    d = r - mu
    var = jnp.mean(d * d, axis=-1, keepdims=True)
    o_ref[...] = d * lax.rsqrt(var + LN_EPS) * lnw_ref[...] + lnb_ref[...]


def _output(x2d, y_a, y_b, p_x2d, w_a, w_b, w_o, ln_w, ln_b, tm=512):
    m = x2d.shape[0]
    row = lambda i: (i, 0)
    const = lambda i: (0, 0)
    return pl.pallas_call(
        _out_kernel,
        grid=(m // tm,),
        in_specs=[
            pl.BlockSpec((tm, D_MODEL), row),
            pl.BlockSpec((tm, D_MODEL), row),
            pl.BlockSpec((tm, D_MODEL), row),
            pl.BlockSpec((tm, HALF), lambda i: (i, COL_GATE_A // HALF)),
            pl.BlockSpec((tm, HALF), lambda i: (i, COL_GATE_A // HALF + 1)),
            pl.BlockSpec((tm, HALF), lambda i: (i, COL_GATE_B // HALF)),
            pl.BlockSpec((tm, HALF), lambda i: (i, COL_GATE_B // HALF + 1)),
            pl.BlockSpec((D_MODEL, D_MODEL), const, pipeline_mode=pl.Buffered(1)),
            pl.BlockSpec((D_MODEL, D_MODEL), const, pipeline_mode=pl.Buffered(1)),
            pl.BlockSpec((D_MODEL, D_MODEL), const, pipeline_mode=pl.Buffered(1)),
            pl.BlockSpec((1, D_MODEL), const),
            pl.BlockSpec((1, D_MODEL), const),
        ],
        out_specs=pl.BlockSpec((tm, D_MODEL), row),
        out_shape=jax.ShapeDtypeStruct((m, D_MODEL), F32),
        scratch_shapes=[pltpu.VMEM((D_MODEL, D_MODEL), BF16)] * 3,
        compiler_params=pltpu.CompilerParams(
            dimension_semantics=("arbitrary",), vmem_limit_bytes=VMEM_LIMIT),
        name="merge_out",
    )(x2d, y_a, y_b, p_x2d, p_x2d, p_x2d, p_x2d, w_a, w_b, w_o, ln_w, ln_b)


def _one_layer(x, meta_tokens, w_in, b_gate, conv_w, a_log, dt_bias, gdn_norm_w, sinks,
               w_proj_a, w_proj_b, w_out, ln_w, ln_b):
    b, seq, _ = x.shape
    w_main, w_bd = _projection_weights(w_in)
    b_gate = b_gate.reshape(1, 2 * D_MODEL)

    x2d = x.reshape(b * seq, D_MODEL)
    meta_chunk = jnp.concatenate([jnp.zeros((META_PAD, D_MODEL), x.dtype), meta_tokens.astype(x.dtype)], axis=0)

    no_halo = jnp.zeros((CONV_HALO, CONV_W), F32)
    p_m, bd_m, meta_tail = _project(meta_chunk, w_main, w_bd, conv_w, b_gate, no_halo,
                                    tm=CHUNK, tiles_per_seq=1)
    p_x2d, bd_x2d, _ = _project(x2d, w_main, w_bd, conv_w, b_gate, meta_tail,
                                tm=PROJ_TM, tiles_per_seq=seq // PROJ_TM)
    p_x = p_x2d.reshape(b, seq, PROJ_W)
    bd_x = bd_x2d.reshape(b, seq, BD_W)

    y_a = _gdn(p_x, p_m, bd_x, bd_m, a_log.reshape(1, GDN_HEADS), dt_bias.reshape(1, GDN_HEADS),
               gdn_norm_w.reshape(1, GDN_D))
    y_b = _swa(p_x, p_m, sinks.reshape(1, SWA_HEADS))

    out = _output(x2d, y_a.reshape(b * seq, GDN_W), y_b.reshape(b * seq, SWA_W), p_x2d,
                  w_proj_a, w_proj_b, w_out,
                  ln_w.reshape(1, D_MODEL), ln_b.reshape(1, D_MODEL))
    return out.reshape(b, seq, D_MODEL)


def kernel(x, meta_tokens, w_in, b_gate, conv_w, a_log, dt_bias, gdn_norm_w, sinks,
           w_proj_a, w_proj_b, w_out, ln_w, ln_b):
    depth = w_in.shape[0]
    assert depth == 1, "meta tokens are projected once; deeper stacks need per-layer meta rows"
    return _one_layer(x, meta_tokens, w_in[0], b_gate[0], conv_w[0], a_log[0], dt_bias[0],
                      gdn_norm_w[0], sinks[0], w_proj_a[0], w_proj_b[0], w_out[0], ln_w[0], ln_b[0])
```

```python
import functools

import jax
import jax.numpy as jnp
from jax import lax
from jax.experimental import pallas as pl
from jax.experimental.pallas import tpu as pltpu

F32 = jnp.float32
BF16 = jnp.bfloat16

D_MODEL = 1024
CHUNK = 64
N_META = 16
META_PAD = CHUNK - N_META
GDN_HEADS = 8
GDN_D = 128
GDN_CONV = 4
GDN_W = GDN_HEADS * GDN_D
SWA_HEADS = 16
SWA_KV_HEADS = 4
SWA_GROUPS = SWA_HEADS // SWA_KV_HEADS
SWA_HD = 64
SWA_W = SWA_HEADS * SWA_HD
SWA_KV_W = SWA_KV_HEADS * SWA_HD
WINDOW_CHUNKS = 2
DEEPNORM_ALPHA = 2.0 ** 0.25
LN_EPS = 1e-5
RMS_EPS = 1e-6
L2_EPS = 1e-6
LOG2E = 1.4426950408889634

COL_Q_A = 0
COL_K_A = GDN_W
COL_V_A = 2 * GDN_W
COL_Z_A = 3 * GDN_W
COL_Q_B = 4 * GDN_W
COL_KV_B = COL_Q_B + SWA_W
COL_Z_B = COL_KV_B + 2 * SWA_KV_W
COL_GATE_A = COL_Z_B + SWA_W
COL_GATE_B = COL_GATE_A + D_MODEL
PROJ_W = COL_GATE_B + D_MODEL
HALF = 512
CONV_W = 3 * GDN_W
BD_W = 128
BD_COLS = 2 * GDN_HEADS

PROJ_TN = 256
PROJ_TM = 256
CONV_HALO = 8
VMEM_LIMIT = 48 * 1024 * 1024
PROJ_VMEM_LIMIT = 56 * 1024 * 1024


def _mm(a, b):
    return jnp.dot(a.astype(BF16), b.astype(BF16), preferred_element_type=F32)


def _mm_nt(a, b):
    return lax.dot_general(a.astype(BF16), b.astype(BF16), (((1,), (1,)), ((), ())),
                           preferred_element_type=F32)


def _mm_tn(a, b):
    return lax.dot_general(a.astype(BF16), b.astype(BF16), (((0,), (0,)), ((), ())),
                           preferred_element_type=F32)


def _mm_f32(a, b):
    return jnp.dot(a, b, preferred_element_type=F32, precision=lax.Precision.HIGHEST)


def _mm_tn_f32(a, b):
    return lax.dot_general(a, b, (((0,), (0,)), ((), ())), preferred_element_type=F32,
                           precision=lax.Precision.HIGHEST)


def _sigmoid(x):
    return 1.0 / (1.0 + jnp.exp2(x * -LOG2E))


SUBLANES = 8


def _shift_rows(a, k):
    rows, width = a.shape
    tiles = a.reshape(rows // SUBLANES, SUBLANES, width)
    rolled = pltpu.roll(tiles, k, axis=1)
    above = jnp.concatenate([rolled[:1], rolled[:-1]], axis=0)
    sub = lax.broadcasted_iota(jnp.int32, tiles.shape, 1)
    return jnp.where(sub < k, above, rolled).reshape(rows, width)


def _silu(x):
    return x * _sigmoid(x)


def _softplus(x):
    return jnp.maximum(x, 0.0) + jnp.log(1.0 + jnp.exp(-jnp.abs(x)))


def _proj_kernel(x_ref, w_ref, wbd_ref, convw_ref, bg_ref, halo0_ref,
                 p_ref, bd_ref, tail_ref, carry_ref, xb_ref, *, tm, tiles_per_seq):
    first_of_seq = (pl.program_id(0) % tiles_per_seq) == 0
    xb_ref[...] = x_ref[...].astype(BF16)
    bd_ref[...] = jnp.dot(xb_ref[...], wbd_ref[...], preferred_element_type=F32)

    n_tiles = PROJ_W // PROJ_TN
    heavy = list(range(CONV_W // PROJ_TN))
    light = list(range(CONV_W // PROJ_TN, n_tiles))
    order = []
    while heavy or light:
        order += heavy[:1] + light[:2]
        heavy, light = heavy[1:], light[2:]

    def matmul(j):
        return jnp.dot(xb_ref[...], w_ref[:, j * PROJ_TN:(j + 1) * PROJ_TN], preferred_element_type=F32)

    def epilogue(j, acc):
        c0 = j * PROJ_TN
        cols = slice(c0, c0 + PROJ_TN)
        if c0 < CONV_W:
            halo = jnp.where(first_of_seq, halo0_ref[:, cols], carry_ref[:, cols])
            tail = acc[tm - CONV_HALO:tm, :]
            carry_ref[:, cols] = tail
            tail_ref[:, cols] = tail
            raw = jnp.concatenate([halo, acc], axis=0)
            w0, w1, w2, w3 = (convw_ref[tap:tap + 1, cols] for tap in range(GDN_CONV))
            back1 = _shift_rows(raw, 1)
            u = raw * w1 + back1 * w0
            y = (acc * w3 + back1[CONV_HALO:, :] * w2) + _shift_rows(u, 2)[CONV_HALO:, :]
            y = _silu(y)
            if c0 < COL_V_A:
                post = GDN_D ** -0.5 if c0 < COL_K_A else 1.0
                groups = []
                for g in range(PROJ_TN // GDN_D):
                    yg = y[:, g * GDN_D:(g + 1) * GDN_D]
                    ss = jnp.sum(yg * yg, axis=-1, keepdims=True)
                    groups.append(yg * (lax.rsqrt(ss + L2_EPS) * post))
                y = jnp.concatenate(groups, axis=1)
        elif c0 < COL_Q_B or COL_Z_B <= c0 < COL_GATE_A:
            y = _silu(acc)
        elif c0 < COL_KV_B:
            y = acc * (LOG2E * SWA_HD ** -0.5)
        elif c0 < COL_Z_B:
            y = acc
        else:
            y = _sigmoid(acc + bg_ref[:, c0 - COL_GATE_A:c0 - COL_GATE_A + PROJ_TN])
        p_ref[:, cols] = y.astype(BF16)

    acc = matmul(order[0])
    for i, j in enumerate(order):
        nxt = matmul(order[i + 1]) if i + 1 < len(order) else None
        epilogue(j, acc)
        acc = nxt


WEIGHT_TN = 512


def _weight_kernel(wt_ref, wt_bd_ref, w_ref, wbd_ref):
    w_ref[...] = wt_ref[...].T.astype(BF16)
    wbd_ref[...] = jnp.zeros_like(wbd_ref)
    wbd_ref[:, 0:BD_COLS] = wt_bd_ref[...].T.astype(BF16)


def _projection_weights(w_in):
    first_q_b = COL_Q_B // WEIGHT_TN
    return pl.pallas_call(
        _weight_kernel,
        grid=(PROJ_W // WEIGHT_TN,),
        in_specs=[
            pl.BlockSpec((pl.Element(WEIGHT_TN), pl.Element(D_MODEL)),
                         lambda j: ((j * (WEIGHT_TN // BD_COLS) + jnp.where(j >= first_q_b, 1, 0)) * BD_COLS, 0)),
            pl.BlockSpec((BD_COLS, D_MODEL), lambda j: (COL_Q_B // BD_COLS, 0)),
        ],
        out_specs=[
            pl.BlockSpec((D_MODEL, WEIGHT_TN), lambda j: (0, j)),
            pl.BlockSpec((D_MODEL, BD_W), lambda j: (0, 0)),
        ],
        out_shape=[
            jax.ShapeDtypeStruct((D_MODEL, PROJ_W), BF16),
            jax.ShapeDtypeStruct((D_MODEL, BD_W), BF16),
        ],
        compiler_params=pltpu.CompilerParams(dimension_semantics=("arbitrary",)),
        name="weight_prep",
    )(jnp.swapaxes(w_in, 0, 1), jnp.swapaxes(w_in, 0, 1))


def _project(x2d, w_main, w_bd, conv_w, b_gate, halo0, tm, tiles_per_seq):
    m = x2d.shape[0]
    n_tiles = m // tm
    const = lambda i: (0, 0)
    resident = dict(pipeline_mode=pl.Buffered(1))
    return pl.pallas_call(
        functools.partial(_proj_kernel, tm=tm, tiles_per_seq=tiles_per_seq),
        grid=(n_tiles,),
        in_specs=[
            pl.BlockSpec((tm, D_MODEL), lambda i: (i, 0)),
            pl.BlockSpec((D_MODEL, PROJ_W), const, **resident),
            pl.BlockSpec((D_MODEL, BD_W), const, **resident),
            pl.BlockSpec((GDN_CONV, CONV_W), const, **resident),
            pl.BlockSpec((1, 2 * D_MODEL), const, **resident),
            pl.BlockSpec((CONV_HALO, CONV_W), const, **resident),
        ],
        out_specs=[
            pl.BlockSpec((tm, PROJ_W), lambda i: (i, 0)),
            pl.BlockSpec((tm, BD_W), lambda i: (i, 0)),
            pl.BlockSpec((CONV_HALO, CONV_W), lambda i: (i, 0)),
        ],
        out_shape=[
            jax.ShapeDtypeStruct((m, PROJ_W), BF16),
            jax.ShapeDtypeStruct((m, BD_W), F32),
            jax.ShapeDtypeStruct((n_tiles * CONV_HALO, CONV_W), F32),
        ],
        scratch_shapes=[pltpu.VMEM((CONV_HALO, CONV_W), F32), pltpu.VMEM((tm, D_MODEL), BF16)],
        compiler_params=pltpu.CompilerParams(
            dimension_semantics=("arbitrary",), vmem_limit_bytes=PROJ_VMEM_LIMIT),
        name="proj",
    )(x2d, w_main, w_bd, conv_w, b_gate, halo0)


GDN_SKEW = 2


GDN_PAIRS = GDN_HEADS // 2


def _gdn_consts(alog_ref, dtb_ref):
    ri = lax.broadcasted_iota(jnp.int32, (CHUNK, 2 * CHUNK), 0)
    lane = lax.broadcasted_iota(jnp.int32, (CHUNK, 2 * CHUNK), 1)
    ci = jnp.bitwise_and(lane, CHUNK - 1)
    levels = []
    s = 2
    while s < CHUNK:
        levels.append(((ri // (2 * s)) == (ci // (2 * s))) & ((ri // s) != (ci // s)))
        s *= 2
    r1 = lax.broadcasted_iota(jnp.int32, (CHUNK, CHUNK), 0)
    c1 = lax.broadcasted_iota(jnp.int32, (CHUNK, CHUNK), 1)
    return dict(tri=ri >= ci, strict=ri > ci, eye=(ri == ci).astype(F32),
                pair=(ri // 2) == (ci // 2), levels=levels, first=lane < CHUNK,
                lower_ones=(r1 >= c1).astype(F32), upper_ones=(r1 <= c1).astype(F32),
                neg_alpha=-jnp.exp(alog_ref[...]), dt_bias=dtb_ref[...])


def _block_diag(first, packed):
    zero = jnp.zeros_like(packed)
    return jnp.concatenate([jnp.where(first, packed, zero), jnp.where(first, zero, packed)], axis=0)


def _stack_diag(m0, m1):
    return jnp.concatenate([jnp.concatenate([m0, jnp.zeros_like(m1)], axis=1),
                            jnp.concatenate([jnp.zeros_like(m0), m1], axis=1)], axis=0)


def _gdn_chunk_stages(chunk, consts, result):
    qkv, bd, row_valid = chunk
    first = consts["first"]
    beta = _sigmoid(bd[:, 0:GDN_HEADS])
    g = consts["neg_alpha"] * _softplus(bd[:, GDN_HEADS:2 * GDN_HEADS] + consts["dt_bias"])
    if row_valid is not None:
        g = g * row_valid
    g_cum = _mm_f32(consts["lower_ones"], g)
    g_cum_t = _mm_tn_f32(g, consts["upper_ones"])
    yield

    heads = []
    for h in range(GDN_HEADS):
        q = qkv(h, 0).astype(F32)
        k_bf = qkv(h, 1)
        k = k_bf.astype(F32)
        v = qkv(h, 2).astype(F32)
        b_col = beta[:, h:h + 1]
        gc_col = g_cum[:, h:h + 1]
        gc_last = g_cum_t[h:h + 1, CHUNK - 1:CHUNK]
        e_col = jnp.exp(gc_col)
        k_beta = k * b_col
        heads.append(dict(
            k=k_bf, k_beta=k_beta.astype(BF16), q=qkv(h, 0),
            rhs=jnp.concatenate([v * b_col, k_beta * e_col], axis=1).astype(BF16),
            q_dec=(q * e_col).astype(BF16),
            k_dec=(k * jnp.exp(gc_last - gc_col)).astype(BF16),
            g_last=jnp.exp(gc_last)))
    pairs = []
    for p in range(GDN_PAIRS):
        h0, h1 = heads[2 * p], heads[2 * p + 1]
        gc_col = jnp.where(first, g_cum[:, 2 * p:2 * p + 1], g_cum[:, 2 * p + 1:2 * p + 2])
        gc_row = jnp.concatenate([g_cum_t[2 * p:2 * p + 1, :], g_cum_t[2 * p + 1:2 * p + 2, :]], axis=1)
        lhs = jnp.concatenate([jnp.concatenate([h0["k_beta"], h1["k_beta"]], axis=1),
                               jnp.concatenate([h0["q"], h1["q"]], axis=1)], axis=0)
        pairs.append(dict(
            decay=jnp.where(consts["tri"], jnp.exp(gc_col - gc_row), 0.0),
            kq=_mm_nt(lhs, _stack_diag(h0["k"], h1["k"]))))
    yield

    for pr in pairs:
        a = jnp.where(consts["strict"], pr["kq"][:CHUNK] * pr["decay"], 0.0)
        pr["a"] = _block_diag(first, a.astype(BF16))
        pr["attn"] = (pr["kq"][CHUNK:] * pr["decay"]).astype(BF16)
        pr["t"] = consts["eye"] - jnp.where(consts["pair"], a, 0.0)
    for off in consts["levels"]:
        for pr in pairs:
            pr["x"] = _mm(pr["t"], pr["a"])
        yield
        for pr in pairs:
            pr["x"] = _mm(pr["x"], _block_diag(first, pr["t"].astype(BF16)))
        yield
        for pr in pairs:
            pr["t"] = pr["t"] - jnp.where(off, pr["x"], 0.0)

    out = dict(u=[], w_q=[], k_dec=[], g_last=[], attn=[pr["attn"] for pr in pairs])
    for p, pr in enumerate(pairs):
        h0, h1 = heads[2 * p], heads[2 * p + 1]
        uw = _mm(pr["t"], _stack_diag(h0["rhs"], h1["rhs"]))
        for i, hd in enumerate((h0, h1)):
            base = 2 * i * GDN_D
            out["u"].append(uw[:, base:base + GDN_D])
            out["w_q"].append(jnp.concatenate([uw[:, base + GDN_D:base + 2 * GDN_D].astype(BF16), hd["q_dec"]],
                                              axis=0))
            out["k_dec"].append(hd["k_dec"])
            out["g_last"].append(hd["g_last"])
    result.update(out)


def _gdn_scan_stages(prep, state, result):
    r = [_mm(w_q, s) for w_q, s in zip(prep["w_q"], state)]
    yield
    v_new = [(u - r_h[:CHUNK]).astype(BF16) for u, r_h in zip(prep["u"], r)]
    av = []
    for p, attn in enumerate(prep["attn"]):
        both = _mm(attn, _stack_diag(v_new[2 * p], v_new[2 * p + 1]))
        av += [both[:, :GDN_D], both[:, GDN_D:]]
    ds = [_mm_tn(k_dec, vn) for k_dec, vn in zip(prep["k_dec"], v_new)]
    result["outs"] = [r_h[CHUNK:] + av_h for r_h, av_h in zip(r, av)]
    result["state"] = [s * g_last + ds_h for g_last, s, ds_h in zip(prep["g_last"], state, ds)]


class _GdnSequence:
    def __init__(self, chunks, state, consts):
        self.n = len(chunks)
        self.state = state
        self.prepared = [{} for _ in range(self.n)]
        self.local = [_gdn_chunk_stages(chunks[c], consts, self.prepared[c]) for c in range(self.n)]
        self.running = [True] * self.n
        self.scan, self.scan_result, self.scan_chunk = None, None, 0

    def advance_local(self, slot):
        for c in range(self.n):
            if self.running[c] and slot >= GDN_SKEW * c:
                self.running[c] = next(self.local[c], "done") != "done"

    def advance_scan(self, on_output):
        if self.scan_chunk == self.n:
            return
        if self.scan is None and not self.running[self.scan_chunk]:
            self.scan_result = {}
            self.scan = _gdn_scan_stages(self.prepared[self.scan_chunk], self.state, self.scan_result)
        if self.scan is not None and next(self.scan, "done") == "done":
            self.state = self.scan_result["state"]
            on_output(self.scan_chunk, self.scan_result["outs"])
            self.scan, self.scan_chunk = None, self.scan_chunk + 1


def _gdn_run(sequences, states, alog_ref, dtb_ref, on_output):
    consts = _gdn_consts(alog_ref, dtb_ref)
    seqs = [_GdnSequence(chunks, state, consts) for chunks, state in zip(sequences, states)]
    slot = 0
    while any(s.scan_chunk < s.n for s in seqs):
        for s in seqs:
            s.advance_local(slot)
        for i, s in enumerate(seqs):
            s.advance_scan(functools.partial(on_output, i))
        slot += 1
    return [s.state for s in seqs]


def _head_cols(part, h):
    col = part * GDN_W + h * GDN_D
    return slice(col, col + GDN_D)


def _gdn_meta_kernel(pm_ref, bdm_ref, alog_ref, dtb_ref, s0_ref):
    rows = lax.broadcasted_iota(jnp.int32, (CHUNK, 1), 0)
    row_valid = (rows >= META_PAD).astype(F32)
    qkv = lambda h, part: pm_ref[:, _head_cols(part, h)]
    (state,) = _gdn_run([[(qkv, bdm_ref[...], row_valid)]], [[jnp.zeros((GDN_D, GDN_D), F32)] * GDN_HEADS],
                        alog_ref, dtb_ref, lambda s, c, outs: None)
    for h in range(GDN_HEADS):
        s0_ref[h] = state[h]


def _gdn_kernel(pa_ref, bdx_ref, s0_ref, alog_ref, dtb_ref, nw_ref, y_ref, s_ref, *, chunks_per_tile, seqs):
    @pl.when(pl.program_id(1) == 0)
    def _():
        for s in range(seqs):
            s_ref[s] = s0_ref[...]

    sequences = []
    for s in range(seqs):
        chunks = []
        for c in range(chunks_per_tile):
            rows = slice(c * CHUNK, (c + 1) * CHUNK)
            qkv = lambda h, part, s=s, rows=rows: pa_ref[s, rows, _head_cols(part, h)]
            chunks.append((qkv, bdx_ref[s, rows, :], None))
        sequences.append(chunks)

    def on_output(s, c, outs):
        rows = slice(c * CHUNK, (c + 1) * CHUNK)
        for h in range(GDN_HEADS):
            o = outs[h]
            o = o * lax.rsqrt(jnp.mean(o * o, axis=-1, keepdims=True) + RMS_EPS) * nw_ref[...]
            z = pa_ref[s, rows, _head_cols(3, h)].astype(F32)
            y_ref[s, rows, h * GDN_D:(h + 1) * GDN_D] = (o * z).astype(BF16)

    states = _gdn_run(sequences, [[s_ref[s, h] for h in range(GDN_HEADS)] for s in range(seqs)],
                      alog_ref, dtb_ref, on_output)
    for s in range(seqs):
        for h in range(GDN_HEADS):
            s_ref[s, h] = states[s][h]


def _gdn(p_x, p_m, bd_x, bd_m, a_log, dt_bias, norm_w, chunks_per_tile=4, seqs=2):
    b, seq, _ = p_x.shape
    tile = chunks_per_tile * CHUNK
    const = lambda *_: (0, 0)
    params = pltpu.CompilerParams(dimension_semantics=("arbitrary", "arbitrary"),
                                  vmem_limit_bytes=VMEM_LIMIT)
    state_shape = (GDN_HEADS, GDN_D, GDN_D)

    s0 = pl.pallas_call(
        _gdn_meta_kernel,
        grid=(1, 1),
        in_specs=[
            pl.BlockSpec((CHUNK, CONV_W), const),
            pl.BlockSpec((CHUNK, BD_W), const),
            pl.BlockSpec((1, GDN_HEADS), const),
            pl.BlockSpec((1, GDN_HEADS), const),
        ],
        out_specs=pl.BlockSpec(state_shape, lambda *_: (0, 0, 0)),
        out_shape=jax.ShapeDtypeStruct(state_shape, F32),
        compiler_params=params,
        name="gdn_meta",
    )(p_m, bd_m, a_log, dt_bias)

    return pl.pallas_call(
        functools.partial(_gdn_kernel, chunks_per_tile=chunks_per_tile, seqs=seqs),
        grid=(b // seqs, seq // tile),
        in_specs=[
            pl.BlockSpec((seqs, tile, 4 * GDN_W), lambda i, t: (i, t, 0)),
            pl.BlockSpec((seqs, tile, BD_W), lambda i, t: (i, t, 0)),
            pl.BlockSpec(state_shape, lambda i, t: (0, 0, 0)),
            pl.BlockSpec((1, GDN_HEADS), const),
            pl.BlockSpec((1, GDN_HEADS), const),
            pl.BlockSpec((1, GDN_D), const),
        ],
        out_specs=pl.BlockSpec((seqs, tile, GDN_W), lambda i, t: (i, t, 0)),
        out_shape=jax.ShapeDtypeStruct((b, seq, GDN_W), BF16),
        scratch_shapes=[pltpu.VMEM((seqs,) + state_shape, F32)],
        compiler_params=params,
        name="gdn",
    )(p_x, bd_x, s0, a_log, dt_bias, norm_w)


SWA_BAND = (WINDOW_CHUNKS + 1) * CHUNK
SWA_KEYS = SWA_BAND + N_META
SWA_ROWS = SWA_GROUPS * CHUNK
LANES = 128
CHUNK_SHIFT = CHUNK.bit_length() - 1
SWA_GROUP_SIZE = 4


def _swa_tables(bias_ref, mterm_ref):
    qi = lax.broadcasted_iota(jnp.int32, (SWA_ROWS, SWA_KEYS), 0)
    kj = lax.broadcasted_iota(jnp.int32, (SWA_ROWS, SWA_KEYS), 1)
    q_in_chunk = jnp.bitwise_and(qi, CHUNK - 1)
    group = lax.shift_right_logical(qi, CHUNK_SHIFT)
    key_chunk = lax.shift_right_logical(kj, CHUNK_SHIFT)
    key_row = jnp.bitwise_and(kj, CHUNK - 1)
    is_meta = kj >= SWA_BAND
    for kvh in range(SWA_KV_HEADS):
        slope = jnp.zeros((SWA_ROWS, SWA_KEYS), F32)
        for g in range(SWA_GROUPS):
            head = kvh * SWA_GROUPS + g
            slope = jnp.where(group == g, LOG2E * 2.0 ** (-8.0 * (head + 1) / SWA_HEADS), slope)
        meta_bias = -slope * (q_in_chunk + N_META - (kj - SWA_BAND)).astype(F32)
        for e in range(WINDOW_CHUNKS + 1):
            d = (e - key_chunk) * CHUNK + q_in_chunk - key_row
            band_bias = jnp.where(key_chunk <= e, -slope * jnp.abs(d).astype(F32), -jnp.inf)
            bias_ref[e, kvh] = jnp.where(is_meta, meta_bias, band_bias)
        mterm_ref[kvh] = jnp.where(is_meta, slope * CHUNK, 0.0)[:, LANES:]


def _swa_kernel(q_ref, z0_ref, z1_ref, kv_ref, kvm_ref, sinks_ref, y_ref, bias_ref, mterm_ref, *,
                chunks_per_tile, seqs):
    t = pl.program_id(1)

    @pl.when((pl.program_id(0) == 0) & (t == 0))
    def _():
        _swa_tables(bias_ref, mterm_ref)

    group = lax.shift_right_logical(lax.broadcasted_iota(jnp.int32, (SWA_ROWS, 1), 0), CHUNK_SHIFT)
    sink_cols = []
    for kvh in range(SWA_KV_HEADS):
        sink = jnp.zeros((SWA_ROWS, 1), F32)
        for g in range(SWA_GROUPS):
            sink = jnp.where(group == g, sinks_ref[0, kvh * SWA_GROUPS + g] * LOG2E, sink)
        sink_cols.append(sink)
    kv_meta = kvm_ref[META_PAD:CHUNK, :]

    def scores(r, c, kvh):
        m = t * chunks_per_tile + c
        k0 = pl.multiple_of(jnp.maximum(m - WINDOW_CHUNKS, 0) * CHUNK, CHUNK)
        kv_win = kv_ref[r, pl.ds(k0, SWA_BAND), :]
        kcol = kvh * SWA_HD
        vcol = SWA_KV_W + kvh * SWA_HD
        k_win = jnp.concatenate([kv_win[:, kcol:kcol + SWA_HD], kv_meta[:, kcol:kcol + SWA_HD]], axis=0)
        v_win = jnp.concatenate([kv_win[:, vcol:vcol + SWA_HD], kv_meta[:, vcol:vcol + SWA_HD]], axis=0)
        rows = slice(c * CHUNK, (c + 1) * CHUNK)
        q_stack = jnp.concatenate(
            [q_ref[r, rows, (kvh * SWA_GROUPS + g) * SWA_HD:(kvh * SWA_GROUPS + g + 1) * SWA_HD]
             for g in range(SWA_GROUPS)], axis=0)
        return _mm_nt(q_stack, k_win), v_win, m

    def softmax(kvh, s, m):
        s = s + bias_ref[jnp.minimum(m, WINDOW_CHUNKS), kvh]
        s = jnp.concatenate([s[:, :LANES], s[:, LANES:] - mterm_ref[kvh] * m.astype(F32)], axis=1)
        sink = sink_cols[kvh]
        mx = jnp.maximum(jnp.max(s, axis=-1, keepdims=True), sink)
        return jnp.exp2(s - mx).astype(BF16), jnp.exp2(sink - mx)

    def store(r, c, kvh, o, key_sum, sink_term):
        o = o * (1.0 / (key_sum[:, :SWA_HD] + sink_term))
        o = jnp.concatenate([o[g * CHUNK:(g + 1) * CHUNK, :] for g in range(SWA_GROUPS)], axis=1)
        rows = slice(c * CHUNK, (c + 1) * CHUNK)
        cols = slice(kvh * SWA_GROUPS * SWA_HD, (kvh + 1) * SWA_GROUPS * SWA_HD)
        z_ref = (z0_ref, z1_ref)[cols.start // HALF]
        z = z_ref[r, rows, cols.start % HALF:cols.start % HALF + SWA_GROUPS * SWA_HD]
        y_ref[r, rows, cols] = (o * z.astype(F32)).astype(BF16)

    ones = jnp.ones((SWA_KEYS, LANES), BF16)

    def finish(group, scored):
        probs = [softmax(kvh, s, m) for (r, c, kvh), (s, _, m) in zip(group, scored)]
        outs = [_mm(e, v_win) for (e, _), (_, v_win, _) in zip(probs, scored)]
        sums = [_mm(e, ones) for e, _ in probs]
        for (r, c, kvh), o, key_sum, (_, sink_term) in zip(group, outs, sums, probs):
            store(r, c, kvh, o, key_sum, sink_term)

    order = [(r, c, kvh) for c in range(chunks_per_tile) for r in range(seqs) for kvh in range(SWA_KV_HEADS)]
    groups = [order[i:i + SWA_GROUP_SIZE] for i in range(0, len(order), SWA_GROUP_SIZE)]
    scored = [scores(*inst) for inst in groups[0]]
    for g, group in enumerate(groups):
        current = scored
        if g + 1 < len(groups):
            scored = [scores(*inst) for inst in groups[g + 1]]
        finish(group, current)


def _swa(p_x, p_m, sinks, chunks_per_tile=2, seqs=2):
    b, seq, _ = p_x.shape
    tile = chunks_per_tile * CHUNK
    return pl.pallas_call(
        functools.partial(_swa_kernel, chunks_per_tile=chunks_per_tile, seqs=seqs),
        grid=(b // seqs, seq // tile),
        in_specs=[
            pl.BlockSpec((seqs, tile, SWA_W), lambda i, t: (i, t, COL_Q_B // SWA_W)),
            pl.BlockSpec((seqs, tile, HALF), lambda i, t: (i, t, COL_Z_B // HALF)),
            pl.BlockSpec((seqs, tile, HALF), lambda i, t: (i, t, COL_Z_B // HALF + 1)),
            pl.BlockSpec((seqs, seq, 2 * SWA_KV_W), lambda i, t: (i, 0, COL_KV_B // (2 * SWA_KV_W))),
            pl.BlockSpec((CHUNK, 2 * SWA_KV_W), lambda i, t: (0, COL_KV_B // (2 * SWA_KV_W))),
            pl.BlockSpec(memory_space=pltpu.SMEM),
        ],
        out_specs=pl.BlockSpec((seqs, tile, SWA_W), lambda i, t: (i, t, 0)),
        out_shape=jax.ShapeDtypeStruct((b, seq, SWA_W), BF16),
        scratch_shapes=[
            pltpu.VMEM((WINDOW_CHUNKS + 1, SWA_KV_HEADS, SWA_ROWS, SWA_KEYS), F32),
            pltpu.VMEM((SWA_KV_HEADS, SWA_ROWS, SWA_KEYS - LANES), F32),
        ],
        compiler_params=pltpu.CompilerParams(
            dimension_semantics=("arbitrary", "arbitrary"), vmem_limit_bytes=VMEM_LIMIT),
        name="swa",
    )(p_x, p_x, p_x, p_x, p_m, sinks)


def _out_kernel(x_ref, ya_ref, yb_ref, ga0_ref, ga1_ref, gb0_ref, gb1_ref, wa_ref, wb_ref, wo_ref,
                lnw_ref, lnb_ref, o_ref, wa_bf, wb_bf, wo_bf):
    @pl.when(pl.program_id(0) == 0)
    def _():
        wa_bf[...] = wa_ref[...].astype(BF16)
        wb_bf[...] = wb_ref[...].astype(BF16)
        wo_bf[...] = wo_ref[...].astype(BF16)

    g_a = jnp.concatenate([ga0_ref[...], ga1_ref[...]], axis=1).astype(F32)
    g_b = jnp.concatenate([gb0_ref[...], gb1_ref[...]], axis=1).astype(F32)
    mixed = g_a * _mm(ya_ref[...], wa_bf[...]) + g_b * _mm(yb_ref[...], wb_bf[...])
    r = DEEPNORM_ALPHA * x_ref[...] + _mm(mixed, wo_bf[...])
    mu = jnp.mean(r, axis=-1, keepdims=True)
    d = r - mu
    var = jnp.mean(d * d, axis=-1, keepdims=True)
    o_ref[...] = d * lax.rsqrt(var + LN_EPS) * lnw_ref[...] + lnb_ref[...]


def _output(x2d, y_a, y_b, p_x2d, w_a, w_b, w_o, ln_w, ln_b, tm=512):
    m = x2d.shape[0]
    row = lambda i: (i, 0)
    const = lambda i: (0, 0)
    return pl.pallas_call(
        _out_kernel,
        grid=(m // tm,),
        in_specs=[
            pl.BlockSpec((tm, D_MODEL), row),
            pl.BlockSpec((tm, D_MODEL), row),
            pl.BlockSpec((tm, D_MODEL), row),
            pl.BlockSpec((tm, HALF), lambda i: (i, COL_GATE_A // HALF)),
            pl.BlockSpec((tm, HALF), lambda i: (i, COL_GATE_A // HALF + 1)),
            pl.BlockSpec((tm, HALF), lambda i: (i, COL_GATE_B // HALF)),
            pl.BlockSpec((tm, HALF), lambda i: (i, COL_GATE_B // HALF + 1)),
            pl.BlockSpec((D_MODEL, D_MODEL), const, pipeline_mode=pl.Buffered(1)),
            pl.BlockSpec((D_MODEL, D_MODEL), const, pipeline_mode=pl.Buffered(1)),
            pl.BlockSpec((D_MODEL, D_MODEL), const, pipeline_mode=pl.Buffered(1)),
            pl.BlockSpec((1, D_MODEL), const),
            pl.BlockSpec((1, D_MODEL), const),
        ],
        out_specs=pl.BlockSpec((tm, D_MODEL), row),
        out_shape=jax.ShapeDtypeStruct((m, D_MODEL), F32),
        scratch_shapes=[pltpu.VMEM((D_MODEL, D_MODEL), BF16)] * 3,
        compiler_params=pltpu.CompilerParams(
            dimension_semantics=("arbitrary",), vmem_limit_bytes=VMEM_LIMIT),
        name="merge_out",
    )(x2d, y_a, y_b, p_x2d, p_x2d, p_x2d, p_x2d, w_a, w_b, w_o, ln_w, ln_b)


def _one_layer(x, meta_tokens, w_in, b_gate, conv_w, a_log, dt_bias, gdn_norm_w, sinks,
               w_proj_a, w_proj_b, w_out, ln_w, ln_b):
    b, seq, _ = x.shape
    w_main, w_bd = _projection_weights(w_in)
    b_gate = b_gate.reshape(1, 2 * D_MODEL)

    x2d = x.reshape(b * seq, D_MODEL)
    meta_chunk = jnp.concatenate([jnp.zeros((META_PAD, D_MODEL), x.dtype), meta_tokens.astype(x.dtype)], axis=0)

    no_halo = jnp.zeros((CONV_HALO, CONV_W), F32)
    p_m, bd_m, meta_tail = _project(meta_chunk, w_main, w_bd, conv_w, b_gate, no_halo,
                                    tm=CHUNK, tiles_per_seq=1)
    p_x2d, bd_x2d, _ = _project(x2d, w_main, w_bd, conv_w, b_gate, meta_tail,
                                tm=PROJ_TM, tiles_per_seq=seq // PROJ_TM)
    p_x = p_x2d.reshape(b, seq, PROJ_W)
    bd_x = bd_x2d.reshape(b, seq, BD_W)

    y_a = _gdn(p_x, p_m, bd_x, bd_m, a_log.reshape(1, GDN_HEADS), dt_bias.reshape(1, GDN_HEADS),
               gdn_norm_w.reshape(1, GDN_D))
    y_b = _swa(p_x, p_m, sinks.reshape(1, SWA_HEADS))

    out = _output(x2d, y_a.reshape(b * seq, GDN_W), y_b.reshape(b * seq, SWA_W), p_x2d,
                  w_proj_a, w_proj_b, w_out,
                  ln_w.reshape(1, D_MODEL), ln_b.reshape(1, D_MODEL))
    return out.reshape(b, seq, D_MODEL)


def kernel(x, meta_tokens, w_in, b_gate, conv_w, a_log, dt_bias, gdn_norm_w, sinks,
           w_proj_a, w_proj_b, w_out, ln_w, ln_b):
    depth = w_in.shape[0]
    assert depth == 1, "meta tokens are projected once; deeper stacks need per-layer meta rows"
    return _one_layer(x, meta_tokens, w_in[0], b_gate[0], conv_w[0], a_log[0], dt_bias[0],
                      gdn_norm_w[0], sinks[0], w_proj_a[0], w_proj_b[0], w_out[0], ln_w[0], ln_b[0])
```
